```python
import math
import jax
import jax.numpy as jnp
from jax import lax
import numpy as np

D_MODEL = 1024
BATCH = 16
SEQ = 2048
DEPTH = 4

GRID_W = 64
CTX_LEN = 256
N_MIXERS = 2
N_S5_LAYERS = (DEPTH + N_MIXERS - 1) // N_MIXERS
N_MLA_LAYERS = DEPTH // N_MIXERS
EPS = 1e-6
ADA_CHUNKS = 6

S5_GROUP_CH = 16
S5_GROUPS = D_MODEL // S5_GROUP_CH
S5_STATE = 64
S5_DT_MIN = 1e-3
S5_DT_MAX = 1e-1

MLA_HEADS = 8
MLA_Q_RANK = 384
MLA_KV_RANK = 256
MLA_NOPE = 128
MLA_ROPE = 64
MLA_V = 128
Q_BLOCK = 128
ROPE_THETA = 10000.0

MOE_GROUPS = 4
MOE_EXPERTS = 8
MOE_TOP_K = 2
MOE_HIDDEN = 512
MOE_N_EXPERTS = MOE_GROUPS * MOE_EXPERTS

kernel_name = 'hybrid_s5_mla_hmoe_dit'


def _rmsnorm(h, g):
    h32 = h.astype(jnp.float32)
    y = h32 * lax.rsqrt(jnp.mean(h32 * h32, axis=-1, keepdims=True) + EPS)
    return (y * g.astype(jnp.float32)).astype(h.dtype)


def _modulate(h, shift, scale):
    return h * (1 + scale) + shift


def _axial_rope_tables(rows, dtype):
    half = MLA_ROPE // 2
    inv_freq = 1.0 / (ROPE_THETA ** (jnp.arange(0, half, 2, dtype=jnp.float32) / half))
    r = jnp.broadcast_to(jnp.arange(rows, dtype=jnp.float32)[:, None], (rows, GRID_W)).reshape(-1)
    col = jnp.broadcast_to(jnp.arange(GRID_W, dtype=jnp.float32)[None, :], (rows, GRID_W)).reshape(-1)
    ang_r = r[:, None] * inv_freq
    ang_c = col[:, None] * inv_freq
    ang = jnp.concatenate([ang_r, ang_r, ang_c, ang_c], axis=-1)
    return jnp.cos(ang).astype(dtype), jnp.sin(ang).astype(dtype)


def _rope2d(t, cos, sin):
    a1, a2, b1, b2 = jnp.split(t, 4, axis=-1)
    rot = jnp.concatenate([-a2, a1, -b2, b1], axis=-1)
    return t * cos + rot * sin


def _complex_linear_combine(e1, e2):
    a1r, a1i, b1r, b1i = e1
    a2r, a2i, b2r, b2i = e2
    return (a2r * a1r - a2i * a1i,
            a2r * a1i + a2i * a1r,
            a2r * b1r - a2i * b1i + b2r,
            a2r * b1i + a2i * b1r + b2i)


def _s5_discretize(a_re, a_im, log_dt, b_re, b_im):
    dt = jnp.exp(log_dt.astype(jnp.float32))[:, None]
    ar = a_re.astype(jnp.float32)
    ai = a_im.astype(jnp.float32)
    ldr, ldi = ar * dt, ai * dt
    mag = jnp.exp(ldr)
    abar_re, abar_im = mag * jnp.cos(ldi), mag * jnp.sin(ldi)
    nr, ni = abar_re - 1.0, abar_im
    den = ar * ar + ai * ai
    qr = (nr * ar + ni * ai) / den
    qi = (ni * ar - nr * ai) / den
    br = b_re.astype(jnp.float32)
    bi = b_im.astype(jnp.float32)
    bbar_re = qr[..., None] * br - qi[..., None] * bi
    bbar_im = qr[..., None] * bi + qi[..., None] * br
    return abar_re, abar_im, bbar_re, bbar_im, ldr, ldi


def _s5_scan(u, abar_re, abar_im, bbar_re, bbar_im, ldr, ldi, reverse, h0):
    n_pos = u.shape[0]
    bu_re = jnp.einsum('lbgc,gpc->lbgp', u, bbar_re)
    bu_im = jnp.einsum('lbgc,gpc->lbgp', u, bbar_im)
    a_re = jnp.broadcast_to(abar_re, (n_pos, 1) + abar_re.shape)
    a_im = jnp.broadcast_to(abar_im, (n_pos, 1) + abar_im.shape)
    _, _, h_re, h_im = lax.associative_scan(
        _complex_linear_combine, (a_re, a_im, bu_re, bu_im), reverse=reverse, axis=0)
    if h0 is not None:
        steps = jnp.arange(n_pos, dtype=jnp.float32)
        n = (n_pos - steps) if reverse else (steps + 1.0)
        n = n[:, None, None, None]
        mag = jnp.exp(ldr * n)
        ang = ldi * n
        p_re, p_im = mag * jnp.cos(ang), mag * jnp.sin(ang)
        h0_re, h0_im = h0
        h_re = h_re + p_re * h0_re - p_im * h0_im
        h_im = h_im + p_re * h0_im + p_im * h0_re
    return h_re, h_im


def _s5_readout(h_re, h_im, c_re, c_im):
    return (jnp.einsum('lbgp,gcp->lbgc', h_re, c_re.astype(jnp.float32))
            - jnp.einsum('lbgp,gcp->lbgc', h_im, c_im.astype(jnp.float32)))


def _s5_mixer(hx, hc, a_re, a_im, log_dt, b_re, b_im, c_re, c_im, d, w_glu, need_ctx_out):
    bsz = hx.shape[0]

    def to_groups(h):
        return jnp.moveaxis(h.astype(jnp.float32).reshape(bsz, h.shape[1], S5_GROUPS, S5_GROUP_CH), 1, 0)

    def from_groups(y):
        return jnp.moveaxis(y, 0, 1).reshape(bsz, y.shape[0], D_MODEL)

    def glu(y, h):
        z = jax.nn.gelu(y + d.astype(jnp.float32) * h.astype(jnp.float32)).astype(h.dtype)
        val, gate = jnp.split(z @ w_glu, 2, axis=-1)
        return val * jax.nn.sigmoid(gate)

    uc, ux = to_groups(hc), to_groups(hx)
    ys_x, ys_c = [], []
    for r, reverse in enumerate((False, True)):
        abr, abi, bbr, bbi, ldr, ldi = _s5_discretize(a_re[r], a_im[r], log_dt[r], b_re[r], b_im[r])
        hc_re, hc_im = _s5_scan(uc, abr, abi, bbr, bbi, ldr, ldi, reverse, None)
        end = 0 if reverse else -1
        hx_re, hx_im = _s5_scan(ux, abr, abi, bbr, bbi, ldr, ldi, reverse, (hc_re[end], hc_im[end]))
        ys_x.append(_s5_readout(hx_re, hx_im, c_re[r], c_im[r]))
        if need_ctx_out:
            ys_c.append(_s5_readout(hc_re, hc_im, c_re[r], c_im[r]))
    out_x = glu(from_groups(ys_x[0] + ys_x[1]), hx)
    out_c = glu(from_groups(ys_c[0] + ys_c[1]), hc) if need_ctx_out else None
    return out_x, out_c


def _attend(qn, qr, kn, kr, v):
    s = jnp.einsum('bqhd,bkhd->bhqk', qn, kn) + jnp.einsum('bqhr,bkr->bhqk', qr, kr)
    p = jax.nn.softmax(s.astype(jnp.float32), axis=-1).astype(v.dtype)
    return jnp.einsum('bhqk,bkhd->bqhd', p, v)


def _attend_blocked(qn, qr, kn, kr, v):
    bsz, n = qn.shape[:2]
    nb = n // Q_BLOCK

    def blocks(q):
        return jnp.moveaxis(q.reshape((bsz, nb, Q_BLOCK) + q.shape[2:]), 1, 0)

    o = lax.map(lambda qs: _attend(qs[0], qs[1], kn, kr, v), (blocks(qn), blocks(qr)))
    return jnp.moveaxis(o, 0, 1).reshape((bsz, n) + o.shape[3:])


def _mla_mixer(hx, hc, w_in, g_q, g_kv, w_uq, w_ukv, w_o, cos, sin, need_ctx_out):
    scale = 1.0 / math.sqrt(MLA_NOPE + MLA_ROPE)
    px, pc = hx @ w_in, hc @ w_in

    def queries(p):
        bsz, n = p.shape[:2]
        cq = _rmsnorm(p[..., :MLA_Q_RANK], g_q)
        q = (cq @ w_uq).reshape(bsz, n, MLA_HEADS, MLA_NOPE + MLA_ROPE)
        return q[..., :MLA_NOPE] * scale, q[..., MLA_NOPE:] * scale

    def keys_values(p):
        bsz, n = p.shape[:2]
        ckv = _rmsnorm(p[..., MLA_Q_RANK:MLA_Q_RANK + MLA_KV_RANK], g_kv)
        kv = (ckv @ w_ukv).reshape(bsz, n, MLA_HEADS, MLA_NOPE + MLA_V)
        return kv[..., :MLA_NOPE], p[..., MLA_Q_RANK + MLA_KV_RANK:], kv[..., MLA_NOPE:]

    kn_c, kr_c, v_c = keys_values(pc)
    kn_x, kr_x, v_x = keys_values(px)
    kr_x = _rope2d(kr_x, cos, sin)
    qn_x, qr_x = queries(px)
    qr_x = _rope2d(qr_x, cos[:, None], sin[:, None])
    kn = jnp.concatenate([kn_c, kn_x], axis=1)
    kr = jnp.concatenate([kr_c, kr_x], axis=1)
    v = jnp.concatenate([v_c, v_x], axis=1)
    o_x = _attend_blocked(qn_x, qr_x, kn, kr, v)
    bsz, n = hx.shape[:2]
    out_x = o_x.reshape(bsz, n, MLA_HEADS * MLA_V) @ w_o
    out_c = None
    if need_ctx_out:
        qn_c, qr_c = queries(pc)
        o_c = _attend(qn_c, qr_c, kn_c, kr_c, v_c)
        out_c = o_c.reshape(bsz, hc.shape[1], MLA_HEADS * MLA_V) @ w_o
    return out_x, out_c


def _hier_moe(h, w_group, b_group, w_expert, b_expert, w_gate_up, w_down):
    shape = h.shape
    t = h.reshape(-1, shape[-1])
    p_group = jax.nn.softmax((t @ w_group + b_group).astype(jnp.float32), axis=-1)
    p_g, g_idx = lax.top_k(p_group, 1)
    logits_e = jnp.einsum('td,gde->tge', t, w_expert) + b_expert
    logits_e = jnp.take_along_axis(logits_e, g_idx[:, :, None], axis=1)[:, 0]
    p_e, e_idx = lax.top_k(jax.nn.softmax(logits_e.astype(jnp.float32), axis=-1), MOE_TOP_K)
    p_e = p_e / jnp.sum(p_e, axis=-1, keepdims=True)
    weights = p_g * p_e
    flat = g_idx * MOE_EXPERTS + e_idx
    combine = jnp.sum(jax.nn.one_hot(flat, MOE_N_EXPERTS, dtype=jnp.float32) * weights[..., None],
                      axis=1).astype(t.dtype)
    y = jnp.zeros_like(t)
    for e in range(MOE_N_EXPERTS):
        gate, up = jnp.split(t @ w_gate_up[e], 2, axis=-1)
        y = y + combine[:, e:e + 1] * ((jax.nn.silu(gate) * up) @ w_down[e])
    return y.reshape(shape)


def setup_inputs(seed: int = 0) -> dict:
    key = jax.random.key(seed)
    ks = iter(jax.random.split(key, 40))
    f32 = jnp.float32

    def nrm(shape, scale):
        return jax.random.normal(next(ks), shape, f32) * scale

    D = D_MODEL
    nA, nB = N_S5_LAYERS, N_MLA_LAYERS
    G, P, C = S5_GROUPS, S5_STATE, S5_GROUP_CH
    H = MLA_HEADS
    inp = {}
    inp['x'] = nrm((BATCH, SEQ, D), 1.0)
    inp['c'] = nrm((BATCH, D), 1.0)
    inp['ctx'] = nrm((BATCH, CTX_LEN, D), 1.0)
    inp['c_ctx'] = nrm((D,), 1.0)
    inp['ada_w'] = nrm((DEPTH, D, ADA_CHUNKS * D), 0.5 * D ** -0.5)
    inp['ada_b'] = nrm((DEPTH, ADA_CHUNKS * D), 0.02)
    inp['norm_mix_g'] = 1.0 + nrm((DEPTH, D), 0.05)
    inp['norm_ffn_g'] = 1.0 + nrm((DEPTH, D), 0.05)
    inp['s5_a_re'] = -0.5 + nrm((nA, 2, G, P), 0.01)
    inp['s5_a_im'] = jnp.broadcast_to(math.pi * jnp.arange(P, dtype=f32), (nA, 2, G, P))
    inp['s5_log_dt'] = jax.random.uniform(next(ks), (nA, 2, G), f32,
                                          math.log(S5_DT_MIN), math.log(S5_DT_MAX))
    inp['s5_b_re'] = nrm((nA, 2, G, P, C), (2 * C) ** -0.5)
    inp['s5_b_im'] = nrm((nA, 2, G, P, C), (2 * C) ** -0.5)
    inp['s5_c_re'] = nrm((nA, 2, G, C, P), P ** -0.5)
    inp['s5_c_im'] = nrm((nA, 2, G, C, P), P ** -0.5)
    inp['s5_d'] = nrm((nA, D), 1.0)
    inp['s5_w_glu'] = nrm((nA, D, 2 * D), D ** -0.5)
    inp['mla_w_in'] = nrm((nB, D, MLA_Q_RANK + MLA_KV_RANK + MLA_ROPE), D ** -0.5)
    inp['mla_g_q'] = 1.0 + nrm((nB, MLA_Q_RANK), 0.05)
    inp['mla_g_kv'] = 1.0 + nrm((nB, MLA_KV_RANK), 0.05)
    inp['mla_w_uq'] = nrm((nB, MLA_Q_RANK, H * (MLA_NOPE + MLA_ROPE)), MLA_Q_RANK ** -0.5)
    inp['mla_w_ukv'] = nrm((nB, MLA_KV_RANK, H * (MLA_NOPE + MLA_V)), MLA_KV_RANK ** -0.5)
    inp['mla_w_o'] = nrm((nB, H * MLA_V, D), (H * MLA_V) ** -0.5)
    inp['moe_w_group'] = nrm((DEPTH, D, MOE_GROUPS), D ** -0.5)
    inp['moe_b_group'] = nrm((DEPTH, MOE_GROUPS), 0.01)
    inp['moe_w_expert'] = nrm((DEPTH, MOE_GROUPS, D, MOE_EXPERTS), D ** -0.5)
    inp['moe_b_expert'] = nrm((DEPTH, MOE_GROUPS, MOE_EXPERTS), 0.01)
    inp['moe_w_gate_up'] = nrm((DEPTH, MOE_N_EXPERTS, D, 2 * MOE_HIDDEN), D ** -0.5)
    inp['moe_w_down'] = nrm((DEPTH, MOE_N_EXPERTS, MOE_HIDDEN, D), MOE_HIDDEN ** -0.5)
    inp['final_g'] = 1.0 + nrm((D,), 0.05)
    return inp


def reference(x, c, ctx, c_ctx, ada_w, ada_b, norm_mix_g, norm_ffn_g,
              s5_a_re, s5_a_im, s5_log_dt, s5_b_re, s5_b_im, s5_c_re, s5_c_im, s5_d, s5_w_glu,
              mla_w_in, mla_g_q, mla_g_kv, mla_w_uq, mla_w_ukv, mla_w_o,
              moe_w_group, moe_b_group, moe_w_expert, moe_b_expert, moe_w_gate_up, moe_w_down,
              final_g):
    seq = x.shape[1]
    rows = seq // GRID_W
    cos, sin = _axial_rope_tables(rows, x.dtype)
    silu_c = jax.nn.silu(c)
    silu_cc = jax.nn.silu(c_ctx)
    for i in range(DEPTH):
        last = i == DEPTH - 1
        mod_x = (silu_c @ ada_w[i] + ada_b[i])[:, None, :]
        mod_c = silu_cc @ ada_w[i] + ada_b[i]
        sh1, sc1, g1, sh2, sc2, g2 = jnp.split(mod_x, ADA_CHUNKS, axis=-1)
        sh1c, sc1c, g1c, sh2c, sc2c, g2c = jnp.split(mod_c, ADA_CHUNKS, axis=-1)
        hx = _modulate(_rmsnorm(x, norm_mix_g[i]), sh1, sc1)
        hc = _modulate(_rmsnorm(ctx, norm_mix_g[i]), sh1c, sc1c)
        j = i // N_MIXERS
        if i % N_MIXERS == 0:
            ox, oc = _s5_mixer(hx, hc, s5_a_re[j], s5_a_im[j], s5_log_dt[j], s5_b_re[j], s5_b_im[j],
                               s5_c_re[j], s5_c_im[j], s5_d[j], s5_w_glu[j], not last)
        else:
            ox, oc = _mla_mixer(hx, hc, mla_w_in[j], mla_g_q[j], mla_g_kv[j], mla_w_uq[j],
                                mla_w_ukv[j], mla_w_o[j], cos, sin, not last)
        x = x + g1 * ox
        hx = _modulate(_rmsnorm(x, norm_ffn_g[i]), sh2, sc2)
        moe_params = (moe_w_group[i], moe_b_group[i], moe_w_expert[i], moe_b_expert[i],
                      moe_w_gate_up[i], moe_w_down[i])
        if last:
            x = x + g2 * _hier_moe(hx, *moe_params)
        else:
            ctx = ctx + g1c * oc
            hc = _modulate(_rmsnorm(ctx, norm_ffn_g[i]), sh2c, sc2c)
            n_ctx = hc.shape[1]
            y = _hier_moe(jnp.concatenate([hc, hx], axis=1), *moe_params)
            ctx = ctx + g2c * y[:, :n_ctx]
            x = x + g2 * y[:, n_ctx:]
    return _rmsnorm(x, final_g)
```

```python
import functools
import math

import jax
import jax.numpy as jnp
from jax import lax
from jax.experimental import pallas as pl
from jax.experimental.pallas import tpu as pltpu

F32 = jnp.float32
BF16 = jnp.bfloat16
I32 = jnp.int32

EPS = 1e-6
GRID_W = 64
ROPE_THETA = 10000.0
ADA_CHUNKS = 6
N_MIXERS = 2
MLA_HEADS = 8
MLA_NOPE = 128
MLA_ROPE = 64
MLA_V = 128
MOE_TOP_K = 2

LANES = 128
MXU_DIM = 256
VMEM_LIMIT = 56 * 1024 * 1024
NEG_BIG = -1e30


def _cparams(n_axes, vmem=VMEM_LIMIT):
    return pltpu.CompilerParams(dimension_semantics=("arbitrary",) * n_axes, vmem_limit_bytes=vmem)


def _sigmoid(x):
    return 1.0 / (1.0 + jnp.exp(-x))


def _rms(x, g):
    return x * lax.rsqrt(jnp.mean(x * x, axis=-1, keepdims=True) + EPS) * g


def _token_tile(c_len, seq):
    t = 256
    while c_len % t or seq % t:
        t //= 2
    return t


def _ada_kernel(cc_ref, w_ref, b_ref, o_ref):
    cc = cc_ref[...]
    s = cc * _sigmoid(cc)
    o_ref[0] = jnp.dot(s, w_ref[0], preferred_element_type=F32, precision=lax.Precision.HIGHEST) + b_ref[0]


def _ada(cc, ada_w, ada_b):
    depth, d, n = ada_w.shape
    bp = cc.shape[0]
    tn = n // 4
    return pl.pallas_call(
        _ada_kernel,
        grid=(depth, n // tn),
        in_specs=[pl.BlockSpec((bp, d), lambda i, j: (0, 0)),
                  pl.BlockSpec((1, d, tn), lambda i, j: (i, 0, j)),
                  pl.BlockSpec((1, 1, tn), lambda i, j: (i, 0, j))],
        out_specs=pl.BlockSpec((1, bp, tn), lambda i, j: (i, 0, j)),
        out_shape=jax.ShapeDtypeStruct((depth, bp, n), F32),
        compiler_params=_cparams(2),
        name="ada_mod",
    )(cc, ada_w, ada_b.reshape(depth, 1, n))


def _mod_spec(chunk, d, n_ctx_tiles):
    return pl.BlockSpec((1, 1, d), lambda b, l: (2 * b + jnp.where(l >= n_ctx_tiles, 1, 0), 0, chunk))


def _norm_mod_kernel(x_ref, g_ref, sh_ref, sc_ref, o_ref):
    y = _rms(x_ref[0], g_ref[...])
    o_ref[...] = (y * (1.0 + sc_ref[0]) + sh_ref[0]).astype(o_ref.dtype).reshape(o_ref.shape)


def _norm_mod(xc, g, modt, chunks, tl, n_ctx_tiles, time_major):
    b, lt, d = xc.shape
    if time_major:
        out_shape = jax.ShapeDtypeStruct((lt, b * d), BF16)
        out_spec = pl.BlockSpec((tl, d), lambda bi, l: (l, bi))
    else:
        out_shape = jax.ShapeDtypeStruct((b, lt, d), BF16)
        out_spec = pl.BlockSpec((1, tl, d), lambda bi, l: (bi, l, 0))
    return pl.pallas_call(
        _norm_mod_kernel,
        grid=(b, lt // tl),
        in_specs=[pl.BlockSpec((1, tl, d), lambda bi, l: (bi, l, 0)),
                  pl.BlockSpec((1, d), lambda bi, l: (0, 0)),
                  _mod_spec(chunks[0], d, n_ctx_tiles),
                  _mod_spec(chunks[1], d, n_ctx_tiles)],
        out_specs=out_spec,
        out_shape=out_shape,
        compiler_params=_cparams(2),
        name="norm_mod",
    )(xc, g.reshape(1, d), modt, modt)


def _final_norm_kernel(x_ref, g_ref, o_ref):
    o_ref[0] = _rms(x_ref[0], g_ref[...])


def _final_norm(xc, g, tl, n_ctx_tiles, seq):
    b, lt, d = xc.shape
    return pl.pallas_call(
        _final_norm_kernel,
        grid=(b, seq // tl),
        in_specs=[pl.BlockSpec((1, tl, d), lambda bi, l: (bi, l + n_ctx_tiles, 0)),
                  pl.BlockSpec((1, d), lambda bi, l: (0, 0))],
        out_specs=pl.BlockSpec((1, tl, d), lambda bi, l: (bi, l, 0)),
        out_shape=jax.ShapeDtypeStruct((b, seq, d), F32),
        compiler_params=_cparams(2),
        name="final_norm",
    )(xc, g.reshape(1, d))


def _s5_disc_kernel(ar_ref, ai_ref, ldt_ref, br_ref, bi_ref, abr_ref, abi_ref, bbr_ref, bbi_ref):
    ar, ai = ar_ref[...], ai_ref[...]
    dt = jnp.exp(ldt_ref[...])
    ldr, ldi = ar * dt, ai * dt
    mag = jnp.exp(ldr)
    abr, abi = mag * jnp.cos(ldi), mag * jnp.sin(ldi)
    nr, ni = abr - 1.0, abi
    den = ar * ar + ai * ai
    qr = (nr * ar + ni * ai) / den
    qi = (ni * ar - nr * ai) / den
    br, bi = br_ref[...], bi_ref[...]
    abr_ref[...] = abr
    abi_ref[...] = abi
    bbr_ref[...] = qr * br - qi * bi
    bbi_ref[...] = qr * bi + qi * br


def _s5_discretize(a_re, a_im, log_dt, b_re, b_im):
    two, g, p, c = b_re.shape
    rows, cols = two * g, p * c
    wide = lambda a: jnp.broadcast_to(a[..., None], (two, g, p, c)).reshape(rows, cols)
    ldt = jnp.broadcast_to(log_dt[:, :, None, None], (two, g, p, c)).reshape(rows, cols)
    spec = pl.BlockSpec((rows, cols), lambda: (0, 0))
    shape = jax.ShapeDtypeStruct((rows, cols), F32)
    abr, abi, bbr, bbi = pl.pallas_call(
        _s5_disc_kernel,
        in_specs=[spec] * 5,
        out_specs=[spec] * 4,
        out_shape=[shape] * 4,
        name="s5_disc",
    )(wide(a_re), wide(a_im), ldt, b_re.reshape(rows, cols), b_im.reshape(rows, cols))
    pick = lambda a: a.reshape(two, g, p, c)[..., 0]
    return pick(abr), pick(abi), bbr.reshape(two, g, p, c), bbi.reshape(two, g, p, c)


def _s5_block_mats(abr, abi, bbr, bbi, c_re, c_im, gpb):
    two, g, p, c = bbr.shape
    ncb = g // gpb
    eye = jnp.eye(gpb, dtype=F32)

    def in_mat(bb):
        bb = bb.reshape(two, ncb, gpb, p, c)
        return jnp.einsum('dngpc,gh->dngchp', bb, eye).reshape(two, ncb, gpb * c, gpb * p)

    def out_mat(cc):
        cc = cc.reshape(two, ncb, gpb, c, p)
        return jnp.einsum('dngcp,gh->dngphc', cc, eye).reshape(two, ncb, gpb * p, gpb * c)

    bm = jnp.concatenate([in_mat(bbr), in_mat(bbi)], axis=-1).astype(BF16)
    cm = jnp.concatenate([out_mat(c_re), -out_mat(c_im)], axis=-2).astype(BF16)
    a = jnp.stack([abr.reshape(two, ncb, gpb * p), abi.reshape(two, ncb, gpb * p)], axis=2)
    return bm, cm, a


def _s5_kernel(xf_ref, xr_ref, bm_ref, cm_ref, a_ref, yf_ref, yr_ref, buf_ref, bur_ref, st_ref,
               *, tc, nb, cw, ns, rb, lw):
    @pl.when(pl.program_id(1) == 0)
    def _():
        st_ref[...] = jnp.zeros_like(st_ref)

    tpb = rb // nb
    for d, (x_ref, y_ref, bu_ref) in enumerate(((xf_ref, yf_ref, buf_ref), (xr_ref, yr_ref, bur_ref))):
        for r in range(tc // tpb):
            x = x_ref[r * tpb:(r + 1) * tpb].reshape(rb, cw)
            bu_ref[r * rb:(r + 1) * rb, :] = jnp.dot(x, bm_ref[d, 0], preferred_element_type=F32)
        for lc in range(ns // lw):
            lo = lc * lw
            are = jnp.broadcast_to(a_ref[d, 0, 0:1, lo:lo + lw], (nb, lw))
            aim = jnp.broadcast_to(a_ref[d, 0, 1:2, lo:lo + lw], (nb, lw))

            def step(i, carry, d=d, lo=lo, are=are, aim=aim, bu_ref=bu_ref):
                hr, hi = carry
                t = i if d == 0 else tc - 1 - i
                rows = pl.ds(pl.multiple_of(t * nb, nb), nb)
                nr = are * hr - aim * hi + bu_ref[rows, lo:lo + lw]
                ni = are * hi + aim * hr + bu_ref[rows, ns + lo:ns + lo + lw]
                bu_ref[rows, lo:lo + lw] = nr
                bu_ref[rows, ns + lo:ns + lo + lw] = ni
                return nr, ni

            hr, hi = lax.fori_loop(0, tc, step, (st_ref[d, 0, :, lo:lo + lw], st_ref[d, 1, :, lo:lo + lw]),
                                   unroll=4)
            st_ref[d, 0, :, lo:lo + lw] = hr
            st_ref[d, 1, :, lo:lo + lw] = hi
        for r in range(tc // tpb):
            h = bu_ref[r * rb:(r + 1) * rb, :].astype(BF16)
            y = jnp.dot(h, cm_ref[d, 0], preferred_element_type=F32)
            y_ref[r * tpb:(r + 1) * tpb] = y.reshape(tpb, nb, cw).astype(y_ref.dtype)


def _s5_scan(ht3, bm, cm, a, c_len, tc):
    lt, nb, d = ht3.shape
    _, ncb, cw, ns2 = bm.shape
    ns = ns2 // 2
    n_c, n_all = c_len // tc, lt // tc
    rb = min(256, tc * nb)

    def rev(k):
        return jnp.where(k < n_c, n_c - 1 - k, n_all - 1 - (k - n_c))

    x_blk = (tc, nb, cw)
    kern = functools.partial(_s5_kernel, tc=tc, nb=nb, cw=cw, ns=ns, rb=rb, lw=min(512, ns))
    return pl.pallas_call(
        kern,
        grid=(ncb, n_all),
        in_specs=[pl.BlockSpec(x_blk, lambda j, k: (k, 0, j)),
                  pl.BlockSpec(x_blk, lambda j, k: (rev(k), 0, j)),
                  pl.BlockSpec((2, 1, cw, ns2), lambda j, k: (0, j, 0, 0)),
                  pl.BlockSpec((2, 1, ns2, cw), lambda j, k: (0, j, 0, 0)),
                  pl.BlockSpec((2, 1, 2, ns), lambda j, k: (0, j, 0, 0))],
        out_specs=[pl.BlockSpec(x_blk, lambda j, k: (k, 0, j)),
                   pl.BlockSpec(x_blk, lambda j, k: (rev(k), 0, j))],
        out_shape=[jax.ShapeDtypeStruct((lt, nb, d), BF16)] * 2,
        scratch_shapes=[pltpu.VMEM((tc * nb, ns2), F32), pltpu.VMEM((tc * nb, ns2), F32),
                        pltpu.VMEM((2, 2, nb, ns), F32)],
        compiler_params=_cparams(2),
        name="s5_scan",
    )(ht3, ht3, bm, cm, a)


def _route_tail(xn, g_ref, sh_ref, sc_ref, wr_ref, br_ref, run_ref, xo_ref, h2_ref, route_ref, cnt_ref,
                *, n_groups, n_exp):
    tl = xn.shape[0]
    xo_ref[0] = xn
    h2 = _rms(xn, g_ref[...]) * (1.0 + sc_ref[0]) + sh_ref[0]
    h2_ref[0] = h2
    logits = jnp.dot(h2, wr_ref[...], preferred_element_type=F32, precision=lax.Precision.HIGHEST) + br_ref[...]
    lane = lax.broadcasted_iota(I32, logits.shape, 1)
    last = LANES - 1

    def masked_softmax(mask):
        lg = jnp.where(mask, logits, NEG_BIG)
        e = jnp.where(mask, jnp.exp(lg - jnp.max(lg, axis=-1, keepdims=True)), 0.0)
        return e / jnp.sum(e, axis=-1, keepdims=True)

    def top1(p, mask):
        v = jnp.max(jnp.where(mask, p, -1.0), axis=-1, keepdims=True)
        idx = jnp.min(jnp.where(mask & (p == v), lane, last), axis=-1, keepdims=True)
        return v, idx

    is_g = lane < n_groups
    p_g, g_idx = top1(masked_softmax(is_g), is_g)
    lo = n_groups + g_idx * n_exp
    is_e = (lane >= lo) & (lane < lo + n_exp)
    pe = masked_softmax(is_e)
    p0, i0 = top1(pe, is_e)
    rest = is_e & (lane != i0)
    p1, i1 = top1(pe, rest)
    den = p0 + p1
    w0, w1 = p_g * (p0 / den), p_g * (p1 / den)
    e0, e1 = i0 - n_groups, i1 - n_groups

    hit0, hit1 = lane == e0, lane == e1
    onehot = jnp.where(hit0 | hit1, 1.0, 0.0)
    row = lax.broadcasted_iota(I32, (tl, tl), 0)
    col = lax.broadcasted_iota(I32, (tl, tl), 1)
    tri = jnp.where(row > col, 1.0, 0.0).astype(BF16)
    before = jnp.dot(tri, onehot.astype(BF16), preferred_element_type=F32) + run_ref[...]
    rank0 = jnp.sum(jnp.where(hit0, before, 0.0), axis=-1, keepdims=True)
    rank1 = jnp.sum(jnp.where(hit1, before, 0.0), axis=-1, keepdims=True)
    run_ref[...] = run_ref[...] + jnp.sum(onehot, axis=0, keepdims=True)
    cnt_ref[...] = jnp.broadcast_to(run_ref[...], cnt_ref.shape)

    vals = (e0.astype(F32), e1.astype(F32), rank0, rank1, w0, w1)
    route = jnp.zeros(logits.shape, F32)
    for i, v in enumerate(vals):
        route = jnp.where(lane == i, v, route)
    route_ref[...] = route


def _tail_specs(b, lt, d, tl, n_ctx_tiles):
    nt = lt // tl
    in_specs = [pl.BlockSpec((1, d), lambda bi, l: (0, 0)),
                _mod_spec(3, d, n_ctx_tiles),
                _mod_spec(4, d, n_ctx_tiles),
                pl.BlockSpec((d, LANES), lambda bi, l: (0, 0)),
                pl.BlockSpec((1, LANES), lambda bi, l: (0, 0))]
    out_specs = [pl.BlockSpec((1, tl, d), lambda bi, l: (bi, l, 0)),
                 pl.BlockSpec((1, tl, d), lambda bi, l: (bi, l, 0)),
                 pl.BlockSpec((tl, LANES), lambda bi, l: (bi * nt + l, 0)),
                 pl.BlockSpec((8, LANES), lambda bi, l: (0, 0))]
    out_shape = [jax.ShapeDtypeStruct((b, lt, d), F32),
                 jax.ShapeDtypeStruct((b, lt, d), F32),
                 jax.ShapeDtypeStruct((b * lt, LANES), F32),
                 jax.ShapeDtypeStruct((8, LANES), F32)]
    return in_specs, out_specs, out_shape


def _init_run(run_ref):
    @pl.when((pl.program_id(0) == 0) & (pl.program_id(1) == 0))
    def _():
        run_ref[...] = jnp.zeros_like(run_ref)


def _glu_kernel(x_ref, yf_ref, yr_ref, h_ref, dd_ref, w_ref, g1_ref,
                g_ref, sh_ref, sc_ref, wr_ref, br_ref,
                xo_ref, h2_ref, route_ref, cnt_ref, run_ref, *, n_groups, n_exp):
    _init_run(run_ref)
    d = x_ref.shape[-1]
    u = yf_ref[...].astype(F32) + yr_ref[...].astype(F32) + dd_ref[...] * h_ref[...].astype(F32)
    z = 0.5 * u * (1.0 + jnp.tanh(math.sqrt(2.0 / math.pi) * (u + 0.044715 * (u * u * u))))
    o = jnp.dot(z.astype(BF16), w_ref[...], preferred_element_type=F32)
    mix = o[:, :d] * _sigmoid(o[:, d:])
    xn = x_ref[0] + g1_ref[0] * mix
    _route_tail(xn, g_ref, sh_ref, sc_ref, wr_ref, br_ref, run_ref, xo_ref, h2_ref, route_ref, cnt_ref,
                n_groups=n_groups, n_exp=n_exp)


def _glu_tail(xc, yft, yrt, ht, s5_d, w_glu, modt, g_ffn, wr, br, tl, n_ctx_tiles, n_groups, n_exp):
    b, lt, d = xc.shape
    tm_spec = pl.BlockSpec((tl, d), lambda bi, l: (l, bi))
    t_in, t_out, t_shape = _tail_specs(b, lt, d, tl, n_ctx_tiles)
    return pl.pallas_call(
        functools.partial(_glu_kernel, n_groups=n_groups, n_exp=n_exp),
        grid=(b, lt // tl),
        in_specs=[pl.BlockSpec((1, tl, d), lambda bi, l: (bi, l, 0)), tm_spec, tm_spec, tm_spec,
                  pl.BlockSpec((1, d), lambda bi, l: (0, 0)),
                  pl.BlockSpec((d, 2 * d), lambda bi, l: (0, 0)),
                  _mod_spec(2, d, n_ctx_tiles)] + t_in,
        out_specs=t_out,
        out_shape=t_shape,
        scratch_shapes=[pltpu.VMEM((1, LANES), F32)],
        input_output_aliases={0: 0},
        compiler_params=_cparams(2),
        name="s5_glu_router",
    )(xc, yft, yrt, ht, s5_d.reshape(1, d), w_glu.astype(BF16), modt,
      g_ffn.reshape(1, d), modt, modt, wr, br)


def _mla_proj_kernel(h_ref, w1_ref, gq_ref, gkv_ref, w2_ref, w2r_ref, wk_ref, wvt_ref, cos_ref, sin_ref,
                     q_ref, k_ref, vt_ref, *, q_rank, kv_rank, n_heads, scale):
    p = jnp.dot(h_ref[0], w1_ref[...], preferred_element_type=F32)
    o_kr = q_rank + kv_rank
    cqn = _rms(p[:, :q_rank], gq_ref[...]).astype(BF16)
    ckvn = _rms(p[:, q_rank:o_kr], gkv_ref[...]).astype(BF16)
    cosp, sinp = cos_ref[...], sin_ref[...]
    kr = (p[:, o_kr:o_kr + LANES] * cosp + p[:, o_kr + LANES:o_kr + 2 * LANES] * sinp).astype(BF16)
    qm = jnp.dot(cqn, w2_ref[...], preferred_element_type=F32)
    qrot = jnp.dot(cqn, w2r_ref[...], preferred_element_type=F32)
    kn = jnp.dot(ckvn, wk_ref[...], preferred_element_type=F32)
    vt = lax.dot_general(wvt_ref[...], ckvn, (((1,), (1,)), ((), ())), preferred_element_type=F32)
    hn = n_heads * MLA_NOPE
    for h in range(n_heads):
        s = slice(h * LANES, (h + 1) * LANES)
        sr = slice(hn + h * LANES, hn + (h + 1) * LANES)
        q_ref[0, h, :, 0:LANES] = (qm[:, s] * scale).astype(BF16)
        q_ref[0, h, :, LANES:2 * LANES] = ((qm[:, sr] * cosp + qrot[:, s] * sinp) * scale).astype(BF16)
        k_ref[0, h, :, 0:LANES] = kn[:, s].astype(BF16)
        k_ref[0, h, :, LANES:2 * LANES] = kr
        vt_ref[0, h] = vt[h * MLA_V:(h + 1) * MLA_V].astype(BF16)


def _rot_cols(w):
    a1, a2, b1, b2 = jnp.split(w, 4, axis=-1)
    return jnp.concatenate([-a2, a1, -b2, b1], axis=-1)


def _pad_lanes(w):
    return jnp.concatenate([w, jnp.zeros(w.shape[:-1] + (LANES - w.shape[-1],), w.dtype)], axis=-1)


def _mla_weights(w_in, w_uq, w_ukv, q_rank, kv_rank):
    h = MLA_HEADS
    o_kr = q_rank + kv_rank
    w_kr = w_in[:, o_kr:]
    w1 = jnp.concatenate([w_in[:, :o_kr], _pad_lanes(w_kr), _pad_lanes(_rot_cols(w_kr))], axis=-1)
    uq = w_uq.reshape(q_rank, h, MLA_NOPE + MLA_ROPE)
    uq_n, uq_r = uq[..., :MLA_NOPE], uq[..., MLA_NOPE:]
    w2 = jnp.concatenate([uq_n.reshape(q_rank, h * MLA_NOPE), _pad_lanes(uq_r).reshape(q_rank, h * LANES)], axis=-1)
    w2r = _pad_lanes(_rot_cols(uq_r)).reshape(q_rank, h * LANES)
    ukv = w_ukv.reshape(kv_rank, h, MLA_NOPE + MLA_V)
    wk = ukv[..., :MLA_NOPE].reshape(kv_rank, h * MLA_NOPE)
    wvt = ukv[..., MLA_NOPE:].reshape(kv_rank, h * MLA_V).T
    return [w.astype(BF16) for w in (w1, w2, w2r, wk, wvt)]


def _rope_tables(c_len, seq):
    half = MLA_ROPE // 2
    inv_freq = 1.0 / (ROPE_THETA ** (jnp.arange(0, half, 2, dtype=F32) / half))
    pos = jnp.arange(seq, dtype=I32)
    ang_r = (pos // GRID_W).astype(F32)[:, None] * inv_freq
    ang_c = (pos % GRID_W).astype(F32)[:, None] * inv_freq
    ang = jnp.concatenate([ang_r, ang_r, ang_c, ang_c], axis=-1)
    ang = jnp.concatenate([jnp.zeros((c_len, MLA_ROPE), F32), ang], axis=0)
    keep = jnp.concatenate([jnp.ones((1, MLA_ROPE), F32), jnp.zeros((1, LANES - MLA_ROPE), F32)], axis=-1)
    return _pad_lanes(jnp.cos(ang)) * keep, _pad_lanes(jnp.sin(ang)) * keep


def _mla_proj(h, weights, g_q, g_kv, cosp, sinp, tl):
    b, lt, d = h.shape
    w1, w2, w2r, wk, wvt = weights
    q_rank, kv_rank = g_q.shape[0], g_kv.shape[0]
    nh = MLA_HEADS
    full = lambda a: pl.BlockSpec(a.shape, lambda bi, l: (0,) * a.ndim)
    scale = 1.0 / math.sqrt(MLA_NOPE + MLA_ROPE)
    return pl.pallas_call(
        functools.partial(_mla_proj_kernel, q_rank=q_rank, kv_rank=kv_rank, n_heads=nh, scale=scale),
        grid=(b, lt // tl),
        in_specs=[pl.BlockSpec((1, tl, d), lambda bi, l: (bi, l, 0)), full(w1),
                  pl.BlockSpec((1, q_rank), lambda bi, l: (0, 0)),
                  pl.BlockSpec((1, kv_rank), lambda bi, l: (0, 0)),
                  full(w2), full(w2r), full(wk), full(wvt),
                  pl.BlockSpec((tl, LANES), lambda bi, l: (l, 0)),
                  pl.BlockSpec((tl, LANES), lambda bi, l: (l, 0))],
        out_specs=[pl.BlockSpec((1, nh, tl, 2 * LANES), lambda bi, l: (bi, 0, l, 0)),
                   pl.BlockSpec((1, nh, tl, 2 * LANES), lambda bi, l: (bi, 0, l, 0)),
                   pl.BlockSpec((1, nh, MLA_V, tl), lambda bi, l: (bi, 0, 0, l))],
        out_shape=[jax.ShapeDtypeStruct((b, nh, lt, 2 * LANES), BF16),
                   jax.ShapeDtypeStruct((b, nh, lt, 2 * LANES), BF16),
                   jax.ShapeDtypeStruct((b, nh, MLA_V, lt), BF16)],
        compiler_params=_cparams(2),
        name="mla_proj",
    )(h, w1, g_q.reshape(1, q_rank), g_kv.reshape(1, kv_rank), w2, w2r, wk, wvt, cosp, sinp)


def _attn_kernel(q_ref, k_ref, vt_ref, o_ref, *, n_ctx_tiles, c_len, n_heads):
    def run(lk):
        def head(h, carry):
            st = lax.dot_general(k_ref[0, h, 0:lk, :], q_ref[0, h], (((1,), (1,)), ((), ())),
                                 preferred_element_type=F32)
            p = jnp.exp(st - jnp.max(st, axis=0, keepdims=True))
            den = jnp.sum(p, axis=0, keepdims=True)
            ot = jnp.dot(vt_ref[0, h, :, 0:lk], p.astype(BF16), preferred_element_type=F32)
            o_ref[0, h] = (ot / den).T.astype(o_ref.dtype)
            return carry

        lax.fori_loop(0, n_heads, head, 0)

    qi = pl.program_id(1)

    @pl.when(qi < n_ctx_tiles)
    def _():
        run(c_len)

    @pl.when(qi >= n_ctx_tiles)
    def _():
        run(k_ref.shape[2])


def _attention(q, k, vt, tl, c_len):
    b, nh, lt, dk = q.shape
    return pl.pallas_call(
        functools.partial(_attn_kernel, n_ctx_tiles=c_len // tl, c_len=c_len, n_heads=nh),
        grid=(b, lt // tl),
        in_specs=[pl.BlockSpec((1, nh, tl, dk), lambda bi, l: (bi, 0, l, 0)),
                  pl.BlockSpec((1, nh, lt, dk), lambda bi, l: (bi, 0, 0, 0)),
                  pl.BlockSpec((1, nh, MLA_V, lt), lambda bi, l: (bi, 0, 0, 0))],
        out_specs=pl.BlockSpec((1, nh, tl, MLA_V), lambda bi, l: (bi, 0, l, 0)),
        out_shape=jax.ShapeDtypeStruct((b, nh, lt, MLA_V), BF16),
        compiler_params=_cparams(2),
        name="mla_attn",
    )(q, k, vt)


def _oproj_kernel(x_ref, o_ref, w_ref, g1_ref, g_ref, sh_ref, sc_ref, wr_ref, br_ref,
                  xo_ref, h2_ref, route_ref, cnt_ref, run_ref, *, n_heads, n_groups, n_exp):
    _init_run(run_ref)
    o = jnp.concatenate([o_ref[0, h] for h in range(n_heads)], axis=-1)
    mix = jnp.dot(o, w_ref[...], preferred_element_type=F32)
    xn = x_ref[0] + g1_ref[0] * mix
    _route_tail(xn, g_ref, sh_ref, sc_ref, wr_ref, br_ref, run_ref, xo_ref, h2_ref, route_ref, cnt_ref,
                n_groups=n_groups, n_exp=n_exp)


def _oproj_tail(xc, o, w_o, modt, g_ffn, wr, br, tl, n_ctx_tiles, n_groups, n_exp):
    b, lt, d = xc.shape
    nh = o.shape[1]
    t_in, t_out, t_shape = _tail_specs(b, lt, d, tl, n_ctx_tiles)
    return pl.pallas_call(
        functools.partial(_oproj_kernel, n_heads=nh, n_groups=n_groups, n_exp=n_exp),
        grid=(b, lt // tl),
        in_specs=[pl.BlockSpec((1, tl, d), lambda bi, l: (bi, l, 0)),
                  pl.BlockSpec((1, nh, tl, MLA_V), lambda bi, l: (bi, 0, l, 0)),
                  pl.BlockSpec(w_o.shape, lambda bi, l: (0, 0)),
                  _mod_spec(2, d, n_ctx_tiles)] + t_in,
        out_specs=t_out,
        out_shape=t_shape,
        scratch_shapes=[pltpu.VMEM((1, LANES), F32)],
        input_output_aliases={0: 0},
        compiler_params=_cparams(2),
        name="mla_out_router",
    )(xc, o, w_o.astype(BF16), modt, g_ffn.reshape(1, d), modt, modt, wr, br)


def _dispatch_kernel(pos_ref, h_ref, xs_in_ref, xs_ref, sem, *, tl):
    del xs_in_ref

    def row_copy(r, dst):
        return pltpu.make_async_copy(h_ref.at[pl.ds(r, 1)], xs_ref.at[pl.ds(dst, 1)], sem)

    def issue(r, carry):
        row_copy(r, pos_ref[0, 0, r]).start()
        row_copy(r, pos_ref[0, 0, tl + r]).start()
        return carry

    def drain(r, carry):
        row_copy(r, pos_ref[0, 0, r]).wait()
        row_copy(r, pos_ref[0, 0, tl + r]).wait()
        return carry

    lax.fori_loop(0, tl, issue, 0)
    lax.fori_loop(0, tl, drain, 0)


def _dispatch(h2, pos_tiles, xs_zero, tl):
    t, d = h2.shape
    return pl.pallas_call(
        functools.partial(_dispatch_kernel, tl=tl),
        grid=(t // tl,),
        in_specs=[pl.BlockSpec((1, 1, 2 * tl), lambda i: (i, 0, 0), memory_space=pltpu.SMEM),
                  pl.BlockSpec((tl, d), lambda i: (i, 0)),
                  pl.BlockSpec(memory_space=pl.ANY)],
        out_specs=pl.BlockSpec(memory_space=pl.ANY),
        out_shape=jax.ShapeDtypeStruct(xs_zero.shape, xs_zero.dtype),
        scratch_shapes=[pltpu.SemaphoreType.DMA(())],
        input_output_aliases={2: 0},
        compiler_params=_cparams(1),
        name="moe_dispatch",
    )(pos_tiles, h2, xs_zero)


def _expert_kernel(te_ref, nu_ref, xs_ref, wgu_ref, wd_ref, ys_ref):
    del te_ref
    used = pl.program_id(0) < nu_ref[0]

    @pl.when(used)
    def _():
        hid = wd_ref.shape[1]
        gu = jnp.dot(xs_ref[...].astype(BF16), wgu_ref[0], preferred_element_type=F32)
        gate, up = gu[:, :hid], gu[:, hid:]
        act = (gate * _sigmoid(gate) * up).astype(BF16)
        ys_ref[...] = jnp.dot(act, wd_ref[0], preferred_element_type=F32)

    @pl.when(jnp.logical_not(used))
    def _():
        ys_ref[...] = jnp.zeros_like(ys_ref)


def _experts(xs, tile_expert, n_used, wgu, wd, tm):
    r, d = xs.shape
    _, _, two_h = wgu.shape
    row = lambda j, te, nu: (jnp.minimum(j, nu[0] - 1), 0)
    return pl.pallas_call(
        _expert_kernel,
        grid_spec=pltpu.PrefetchScalarGridSpec(
            num_scalar_prefetch=2,
            grid=(r // tm,),
            in_specs=[pl.BlockSpec((tm, d), row),
                      pl.BlockSpec((1, d, two_h), lambda j, te, nu: (te[j], 0, 0)),
                      pl.BlockSpec((1, two_h // 2, d), lambda j, te, nu: (te[j], 0, 0))],
            out_specs=pl.BlockSpec((tm, d), lambda j, te, nu: (j, 0))),
        out_shape=jax.ShapeDtypeStruct((r, d), F32),
        compiler_params=_cparams(1),
        name="moe_experts",
    )(tile_expert, n_used, xs, wgu, wd)


def _combine_kernel(pos_ref, ys_ref, route_ref, x_ref, g2_ref, o_ref, y0_ref, y1_ref, sem, *, tl):
    def row_copy(src, buf, r):
        return pltpu.make_async_copy(ys_ref.at[pl.ds(src, 1)], buf.at[pl.ds(r, 1)], sem)

    def issue(r, carry):
        row_copy(pos_ref[0, 0, r], y0_ref, r).start()
        row_copy(pos_ref[0, 0, tl + r], y1_ref, r).start()
        return carry

    def drain(r, carry):
        row_copy(pos_ref[0, 0, r], y0_ref, r).wait()
        row_copy(pos_ref[0, 0, tl + r], y1_ref, r).wait()
        return carry

    lax.fori_loop(0, tl, issue, 0)
    lax.fori_loop(0, tl, drain, 0)
    route = route_ref[...]
    y = route[:, 4:5] * y0_ref[...] + route[:, 5:6] * y1_ref[...]
    o_ref[0] = x_ref[0] + g2_ref[0] * y


def _combine(xc, ys, pos_tiles, route, modt, tl, n_ctx_tiles):
    b, lt, d = xc.shape
    nt = lt // tl
    return pl.pallas_call(
        functools.partial(_combine_kernel, tl=tl),
        grid=(b, nt),
        in_specs=[pl.BlockSpec((1, 1, 2 * tl), lambda bi, l: (bi * nt + l, 0, 0), memory_space=pltpu.SMEM),
                  pl.BlockSpec(memory_space=pl.ANY),
                  pl.BlockSpec((tl, LANES), lambda bi, l: (bi * nt + l, 0)),
                  pl.BlockSpec((1, tl, d), lambda bi, l: (bi, l, 0)),
                  _mod_spec(5, d, n_ctx_tiles)],
        out_specs=pl.BlockSpec((1, tl, d), lambda bi, l: (bi, l, 0)),
        out_shape=jax.ShapeDtypeStruct((b, lt, d), F32),
        scratch_shapes=[pltpu.VMEM((tl, d), F32), pltpu.VMEM((tl, d), F32), pltpu.SemaphoreType.DMA(())],
        input_output_aliases={3: 0},
        compiler_params=_cparams(2),
        name="moe_combine",
    )(pos_tiles, ys, route, xc, modt)


def _moe(xc, h2, route, counts, wgu, wd, modt, tl, tm, n_ctx_tiles):
    b, lt, d = xc.shape
    t = b * lt
    n_exp_total = wgu.shape[0]
    n_tiles = (MOE_TOP_K * t) // tm + n_exp_total
    eid = route[:, 0:MOE_TOP_K].astype(I32)
    rank = route[:, 2:2 + MOE_TOP_K].astype(I32)
    cnt = counts[0, :n_exp_total].astype(I32)
    padded = ((cnt + tm - 1) // tm) * tm
    seg_end = jnp.cumsum(padded)
    seg_start = seg_end - padded
    pos = seg_start[eid] + rank
    n_used = (seg_end[-1] // tm).astype(I32)
    tile_start = jnp.arange(n_tiles, dtype=I32) * tm
    tile_expert = jnp.searchsorted(seg_end, tile_start, side='right').astype(I32)
    tile_expert = jnp.where(jnp.arange(n_tiles) < n_used, tile_expert, tile_expert[n_used - 1])
    pos_tiles = jnp.concatenate([pos[:, k].reshape(t // tl, 1, tl) for k in range(MOE_TOP_K)], axis=-1)

    xs = _dispatch(h2.reshape(t, d), pos_tiles, jnp.zeros((n_tiles * tm, d), F32), tl)
    ys = _experts(xs, tile_expert, n_used.reshape(1), wgu, wd, tm)
    return _combine(xc, ys, pos_tiles, route, modt, tl, n_ctx_tiles)


def _router_weights(w_group, b_group, w_expert, b_expert):
    d, n_groups = w_group.shape
    n_exp = w_expert.shape[-1]
    w = jnp.concatenate([w_group, jnp.transpose(w_expert, (1, 0, 2)).reshape(d, n_groups * n_exp)], axis=-1)
    bias = jnp.concatenate([b_group, b_expert.reshape(-1)])
    return _pad_lanes(w), _pad_lanes(bias[None, :])


def kernel(x, c, ctx, c_ctx, ada_w, ada_b, norm_mix_g, norm_ffn_g, s5_a_re, s5_a_im, s5_log_dt, s5_b_re, s5_b_im, s5_c_re, s5_c_im, s5_d, s5_w_glu, mla_w_in, mla_g_q, mla_g_kv, mla_w_uq, mla_w_ukv, mla_w_o, moe_w_group, moe_b_group, moe_w_expert, moe_b_expert, moe_w_gate_up, moe_w_down, final_g):
    b, seq, d = x.shape
    c_len = ctx.shape[1]
    depth = ada_w.shape[0]
    n_groups, n_exp = moe_w_group.shape[-1], moe_w_expert.shape[-1]
    assert n_groups + n_groups * n_exp <= LANES and seq % GRID_W == 0
    tl = _token_tile(c_len, seq)
    n_ctx_tiles = c_len // tl
    tm = tl
    tc = min(64, tl)
    gpb = MXU_DIM // s5_b_re.shape[-1]

    xc = jnp.concatenate([ctx, x], axis=1)
    bp = -(-(b + 1) // 8) * 8
    cc = jnp.concatenate([c, c_ctx[None, :], jnp.zeros((bp - b - 1, d), F32)], axis=0)
    mod = _ada(cc, ada_w, ada_b)
    cosp, sinp = _rope_tables(c_len, seq)
    wgu_all, wd_all = moe_w_gate_up.astype(BF16), moe_w_down.astype(BF16)

    for i in range(depth):
        j = i // N_MIXERS
        mod_c = jnp.broadcast_to(mod[i, b][None, :], (b, ADA_CHUNKS * d))
        modt = jnp.stack([mod_c, mod[i, :b]], axis=1).reshape(2 * b, 1, ADA_CHUNKS * d)
        wr, br = _router_weights(moe_w_group[i], moe_b_group[i], moe_w_expert[i], moe_b_expert[i])
        if i % N_MIXERS == 0:
            ht = _norm_mod(xc, norm_mix_g[i], modt, (0, 1), tl, n_ctx_tiles, time_major=True)
            abr, abi, bbr, bbi = _s5_discretize(s5_a_re[j], s5_a_im[j], s5_log_dt[j], s5_b_re[j], s5_b_im[j])
            bm, cm, a = _s5_block_mats(abr, abi, bbr, bbi, s5_c_re[j], s5_c_im[j], gpb)
            yf, yr = _s5_scan(ht.reshape(c_len + seq, b, d), bm, cm, a, c_len, tc)
            flat = lambda y: y.reshape(c_len + seq, b * d)
            xc, h2, route, counts = _glu_tail(xc, flat(yf), flat(yr), ht, s5_d[j], s5_w_glu[j], modt,
                                              norm_ffn_g[i], wr, br, tl, n_ctx_tiles, n_groups, n_exp)
        else:
            h = _norm_mod(xc, norm_mix_g[i], modt, (0, 1), tl, n_ctx_tiles, time_major=False)
            weights = _mla_weights(mla_w_in[j], mla_w_uq[j], mla_w_ukv[j], mla_g_q.shape[-1], mla_g_kv.shape[-1])
            q, k, vt = _mla_proj(h, weights, mla_g_q[j], mla_g_kv[j], cosp, sinp, tl)
            o = _attention(q, k, vt, tl, c_len)
            xc, h2, route, counts = _oproj_tail(xc, o, mla_w_o[j], modt, norm_ffn_g[i], wr, br,
                                                tl, n_ctx_tiles, n_groups, n_exp)
        xc = _moe(xc, h2, route, counts, wgu_all[i], wd_all[i], modt, tl, tm, n_ctx_tiles)
    return _final_norm(xc, final_g, tl, n_ctx_tiles, seq)
```

```python
import functools
import math

import jax
import jax.numpy as jnp
import numpy as np
from jax import lax
from jax.experimental import pallas as pl
from jax.experimental.pallas import tpu as pltpu

F32 = jnp.float32
BF16 = jnp.bfloat16
I32 = jnp.int32

EPS = 1e-6
GRID_W = 64
ROPE_THETA = 10000.0
ADA_CHUNKS = 6
N_MIXERS = 2
MLA_HEADS = 8
MLA_NOPE = 128
MLA_ROPE = 64
MLA_V = 128
MOE_TOP_K = 2

LANES = 128
SUBLANES = 8
MXU_DIM = 256
VMEM_LIMIT = 56 * 1024 * 1024
NEG_BIG = -1e30
NT_DIMS = (((1,), (1,)), ((), ()))
MOE_TILE = 128
ATTN_KEY_CHUNK = 256


def _cparams(n_axes, vmem=VMEM_LIMIT):
    return pltpu.CompilerParams(dimension_semantics=("arbitrary",) * n_axes, vmem_limit_bytes=vmem)


def _sigmoid(x):
    return 1.0 / (1.0 + jnp.exp(-x))


def _rms(x, g):
    return x * lax.rsqrt(jnp.mean(x * x, axis=-1, keepdims=True) + EPS) * g


def _token_tile(c_len, seq):
    t = 256
    while c_len % t or seq % t:
        t //= 2
    return t


def _pad_lanes(w):
    return jnp.concatenate([w, jnp.zeros(w.shape[:-1] + (LANES - w.shape[-1],), w.dtype)], axis=-1)


def _ada_kernel(cc_ref, w_ref, b_ref, o_ref):
    cc = cc_ref[...]
    s = cc * _sigmoid(cc)
    o_ref[0] = jnp.dot(s, w_ref[0], preferred_element_type=F32, precision=lax.Precision.HIGHEST) + b_ref[0]


def _ada(cc, ada_w, ada_b):
    depth, d, n = ada_w.shape
    bp = cc.shape[0]
    tn = n // 4
    return pl.pallas_call(
        _ada_kernel,
        grid=(depth, n // tn),
        in_specs=[pl.BlockSpec((bp, d), lambda i, j: (0, 0)),
                  pl.BlockSpec((1, d, tn), lambda i, j: (i, 0, j)),
                  pl.BlockSpec((1, 1, tn), lambda i, j: (i, 0, j))],
        out_specs=pl.BlockSpec((1, bp, tn), lambda i, j: (i, 0, j)),
        out_shape=jax.ShapeDtypeStruct((depth, bp, n), F32),
        compiler_params=_cparams(2),
        name="ada_mod",
    )(cc, ada_w, ada_b.reshape(depth, 1, n))


def _mod_spec(chunk, d, n_ctx_tiles, shift=0):
    return pl.BlockSpec((1, 1, d), lambda b, l: (2 * b + jnp.where(l + shift >= n_ctx_tiles, 1, 0), 0, chunk))


def _norm_mod_kernel(x_ref, g_ref, sh_ref, sc_ref, o_ref):
    y = _rms(x_ref[0], g_ref[...])
    o_ref[...] = (y * (1.0 + sc_ref[0]) + sh_ref[0]).astype(o_ref.dtype).reshape(o_ref.shape)


def _res_norm_mod_kernel(x_ref, y_ref, g2_ref, g_ref, sh_ref, sc_ref, xo_ref, o_ref):
    xn = x_ref[0] + g2_ref[0] * y_ref[...]
    xo_ref[0] = xn
    o_ref[...] = (_rms(xn, g_ref[...]) * (1.0 + sc_ref[0]) + sh_ref[0]).astype(o_ref.dtype).reshape(o_ref.shape)


def _norm_mod(xc, g, modt, tl, n_ctx_tiles, time_major, moe=None):
    b, lt, d = xc.shape
    nt = lt // tl
    tok = pl.BlockSpec((1, tl, d), lambda bi, l: (bi, l, 0))
    if time_major:
        h_shape = jax.ShapeDtypeStruct((lt, b * d), BF16)
        h_spec = pl.BlockSpec((tl, d), lambda bi, l: (l, bi))
    else:
        h_shape, h_spec = jax.ShapeDtypeStruct((b, lt, d), BF16), tok
    tail_specs = [pl.BlockSpec((1, d), lambda bi, l: (0, 0)),
                  _mod_spec(0, d, n_ctx_tiles), _mod_spec(1, d, n_ctx_tiles)]
    if moe is None:
        h = pl.pallas_call(
            _norm_mod_kernel, grid=(b, nt), in_specs=[tok] + tail_specs, out_specs=h_spec, out_shape=h_shape,
            compiler_params=_cparams(2), name="norm_mod",
        )(xc, g.reshape(1, d), modt, modt)
        return xc, h
    y, prev_modt = moe
    return pl.pallas_call(
        _res_norm_mod_kernel, grid=(b, nt),
        in_specs=[tok, pl.BlockSpec((tl, d), lambda bi, l: (bi * nt + l, 0)), _mod_spec(5, d, n_ctx_tiles)] + tail_specs,
        out_specs=[tok, h_spec],
        out_shape=[jax.ShapeDtypeStruct((b, lt, d), F32), h_shape],
        input_output_aliases={0: 0},
        compiler_params=_cparams(2), name="moe_res_norm_mod",
    )(xc, y, prev_modt, g.reshape(1, d), modt, modt)


def _final_norm_kernel(x_ref, y_ref, g2_ref, g_ref, o_ref):
    o_ref[0] = _rms(x_ref[0] + g2_ref[0] * y_ref[...], g_ref[...])


def _final_norm(xc, y, modt, g, tl, n_ctx_tiles, seq):
    b, lt, d = xc.shape
    nt = lt // tl
    return pl.pallas_call(
        _final_norm_kernel,
        grid=(b, seq // tl),
        in_specs=[pl.BlockSpec((1, tl, d), lambda bi, l: (bi, l + n_ctx_tiles, 0)),
                  pl.BlockSpec((tl, d), lambda bi, l: (bi * nt + l + n_ctx_tiles, 0)),
                  _mod_spec(5, d, n_ctx_tiles, shift=n_ctx_tiles),
                  pl.BlockSpec((1, d), lambda bi, l: (0, 0))],
        out_specs=pl.BlockSpec((1, tl, d), lambda bi, l: (bi, l, 0)),
        out_shape=jax.ShapeDtypeStruct((b, seq, d), F32),
        compiler_params=_cparams(2),
        name="final_norm",
    )(xc, y, modt, g.reshape(1, d))


def _s5_disc_kernel(ar_ref, ai_ref, ldt_ref, br_ref, bi_ref, abr_ref, abi_ref, bbr_ref, bbi_ref):
    ar, ai = ar_ref[...], ai_ref[...]
    dt = jnp.exp(ldt_ref[...])
    ldr, ldi = ar * dt, ai * dt
    mag = jnp.exp(ldr)
    abr, abi = mag * jnp.cos(ldi), mag * jnp.sin(ldi)
    nr, ni = abr - 1.0, abi
    den = ar * ar + ai * ai
    qr = (nr * ar + ni * ai) / den
    qi = (ni * ar - nr * ai) / den
    br, bi = br_ref[...], bi_ref[...]
    abr_ref[...] = abr
    abi_ref[...] = abi
    bbr_ref[...] = qr * br - qi * bi
    bbi_ref[...] = qr * bi + qi * br


def _s5_discretize(a_re, a_im, log_dt, b_re, b_im):
    two, g, p, c = b_re.shape
    rows, cols = two * g, p * c
    wide = lambda a: jnp.broadcast_to(a[..., None], (two, g, p, c)).reshape(rows, cols)
    ldt = jnp.broadcast_to(log_dt[:, :, None, None], (two, g, p, c)).reshape(rows, cols)
    spec = pl.BlockSpec((rows, cols), lambda: (0, 0))
    shape = jax.ShapeDtypeStruct((rows, cols), F32)
    abr, abi, bbr, bbi = pl.pallas_call(
        _s5_disc_kernel,
        in_specs=[spec] * 5,
        out_specs=[spec] * 4,
        out_shape=[shape] * 4,
        name="s5_disc",
    )(wide(a_re), wide(a_im), ldt, b_re.reshape(rows, cols), b_im.reshape(rows, cols))
    pick = lambda a: a.reshape(two, g, p, c)[..., 0]
    return pick(abr), pick(abi), bbr.reshape(two, g, p, c), bbi.reshape(two, g, p, c)


def _s5_block_mats(abr, abi, bbr, bbi, c_re, c_im, gpb):
    two, g, p, c = bbr.shape
    ncb = g // gpb
    eye = jnp.eye(gpb, dtype=F32)

    def in_mat(bb):
        bb = bb.reshape(two, ncb, gpb, p, c)
        return jnp.einsum('dngpc,gh->dngchp', bb, eye).reshape(two, ncb, gpb * c, gpb * p)

    def out_mat(cc):
        cc = cc.reshape(two, ncb, gpb, c, p)
        return jnp.einsum('dngcp,gh->dngphc', cc, eye).reshape(two, ncb, gpb * p, gpb * c)

    bm = jnp.concatenate([in_mat(bbr), in_mat(bbi)], axis=-1).astype(BF16)
    cm = jnp.concatenate([out_mat(c_re), -out_mat(c_im)], axis=-2).astype(BF16)
    a = jnp.stack([abr.reshape(two, ncb, gpb * p), abi.reshape(two, ncb, gpb * p)], axis=2)
    return bm, cm, a


def _s5_kernel(xf_ref, xr_ref, bm_ref, cm_ref, a_ref, yf_ref, yr_ref, buf_ref, bur_ref, st_ref,
               *, tc, nb, cw, ns, rb, lw):
    @pl.when(pl.program_id(1) == 0)
    def _():
        st_ref[...] = jnp.zeros_like(st_ref)

    tpb = rb // nb
    nrb = tc // tpb
    refs = ((xf_ref, yf_ref, buf_ref), (xr_ref, yr_ref, bur_ref))
    order = (tuple(range(nrb)), tuple(range(nrb - 1, -1, -1)))

    def inject(d, r):
        x_ref, _, bu_ref = refs[d]
        x = x_ref[r * tpb:(r + 1) * tpb].reshape(rb, cw)
        bu_ref[r * rb:(r + 1) * rb, :] = jnp.dot(x, bm_ref[d, 0], preferred_element_type=F32)

    def recur(d, r):
        bu_ref = refs[d][2]
        for lc in range(ns // lw):
            lo = lc * lw
            are = jnp.broadcast_to(a_ref[d, 0, 0:1, lo:lo + lw], (nb, lw))
            aim = jnp.broadcast_to(a_ref[d, 0, 1:2, lo:lo + lw], (nb, lw))
            hr, hi = st_ref[d, 0, :, lo:lo + lw], st_ref[d, 1, :, lo:lo + lw]
            for i in range(tpb):
                t = r * tpb + (i if d == 0 else tpb - 1 - i)
                rows = slice(t * nb, (t + 1) * nb)
                hr, hi = (are * hr - aim * hi + bu_ref[rows, lo:lo + lw],
                          are * hi + aim * hr + bu_ref[rows, ns + lo:ns + lo + lw])
                bu_ref[rows, lo:lo + lw] = hr
                bu_ref[rows, ns + lo:ns + lo + lw] = hi
            st_ref[d, 0, :, lo:lo + lw] = hr
            st_ref[d, 1, :, lo:lo + lw] = hi

    def readout(d, r):
        _, y_ref, bu_ref = refs[d]
        h = bu_ref[r * rb:(r + 1) * rb, :].astype(BF16)
        y = jnp.dot(h, cm_ref[d, 0], preferred_element_type=F32)
        y_ref[r * tpb:(r + 1) * tpb] = y.reshape(tpb, nb, cw).astype(y_ref.dtype)

    for d in range(2):
        inject(d, order[d][0])
    for k in range(nrb):
        for d in range(2):
            if k + 1 < nrb:
                inject(d, order[d][k + 1])
        for d in range(2):
            recur(d, order[d][k])
        for d in range(2):
            if k >= 1:
                readout(d, order[d][k - 1])
    for d in range(2):
        readout(d, order[d][nrb - 1])


def _s5_scan(ht3, bm, cm, a, c_len, tc):
    lt, nb, d = ht3.shape
    _, ncb, cw, ns2 = bm.shape
    ns = ns2 // 2
    n_c, n_all = c_len // tc, lt // tc
    rb = min(256, tc * nb)

    def rev(k):
        return jnp.where(k < n_c, n_c - 1 - k, n_all - 1 - (k - n_c))

    x_blk = (tc, nb, cw)
    kern = functools.partial(_s5_kernel, tc=tc, nb=nb, cw=cw, ns=ns, rb=rb, lw=min(256, ns))
    return pl.pallas_call(
        kern,
        grid=(ncb, n_all),
        in_specs=[pl.BlockSpec(x_blk, lambda j, k: (k, 0, j)),
                  pl.BlockSpec(x_blk, lambda j, k: (rev(k), 0, j)),
                  pl.BlockSpec((2, 1, cw, ns2), lambda j, k: (0, j, 0, 0)),
                  pl.BlockSpec((2, 1, ns2, cw), lambda j, k: (0, j, 0, 0)),
                  pl.BlockSpec((2, 1, 2, ns), lambda j, k: (0, j, 0, 0))],
        out_specs=[pl.BlockSpec(x_blk, lambda j, k: (k, 0, j)),
                   pl.BlockSpec(x_blk, lambda j, k: (rev(k), 0, j))],
        out_shape=[jax.ShapeDtypeStruct((lt, nb, d), BF16)] * 2,
        scratch_shapes=[pltpu.VMEM((tc * nb, ns2), F32), pltpu.VMEM((tc * nb, ns2), F32),
                        pltpu.VMEM((2, 2, nb, ns), F32)],
        compiler_params=_cparams(2),
        name="s5_scan",
    )(ht3, ht3, bm, cm, a)


def _route_tail(xn, g_ref, sh_ref, sc_ref, wrh_ref, wrl_ref, br_ref, run_ref, xo_ref, h2_ref, route_ref, cnt_ref,
                *, n_groups, n_exp):
    tl, d = xn.shape
    xo_ref[0] = xn
    h2 = _rms(xn, g_ref[...]) * (1.0 + sc_ref[0]) + sh_ref[0]
    h_hi = h2.astype(BF16)
    h_lo = (h2 - h_hi.astype(F32)).astype(BF16)
    dg = lambda w, x: lax.dot_general(w, x, NT_DIMS, preferred_element_type=F32)
    logits = dg(wrh_ref[...], h_hi) + (dg(wrh_ref[...], h_lo) + dg(wrl_ref[...], h_hi)) + br_ref[...]

    row = lax.broadcasted_iota(I32, (SUBLANES, tl), 0)

    def softmax0(v, mask):
        v = jnp.where(mask, v, NEG_BIG)
        e = jnp.where(mask, jnp.exp(v - jnp.max(v, axis=0, keepdims=True)), 0.0)
        return e / jnp.sum(e, axis=0, keepdims=True)

    def top1(p, mask):
        v = jnp.max(jnp.where(mask, p, -1.0), axis=0, keepdims=True)
        idx = jnp.min(jnp.where(mask & (p == v), row, SUBLANES), axis=0, keepdims=True)
        return v, idx

    is_g = row < n_groups
    p_g, g_idx = top1(softmax0(logits[0:SUBLANES], is_g), is_g)
    le = logits[SUBLANES:2 * SUBLANES]
    for g in range(1, n_groups):
        le = jnp.where(g_idx == g, logits[(g + 1) * SUBLANES:(g + 2) * SUBLANES], le)
    every = row >= 0
    pe = softmax0(le, every)
    p0, i0 = top1(pe, every)
    p1, i1 = top1(pe, row != i0)
    den = p0 + p1
    w0, w1 = p_g * (p0 / den), p_g * (p1 / den)
    first_low = i0 < i1
    ea, eb = jnp.minimum(i0, i1), jnp.maximum(i0, i1)
    wa, wb = jnp.where(first_low, w0, w1), jnp.where(first_low, w1, w0)
    n_pairs = n_exp * (n_exp - 1) // 2
    pair = ea * (n_exp - 1) - jnp.right_shift(ea * (ea - 1), 1) + (eb - ea - 1)
    cls = g_idx * n_pairs + pair

    crow = lax.broadcasted_iota(I32, (LANES, tl), 0)
    hit = crow == cls
    onehot = jnp.where(hit, 1.0, 0.0)
    src = lax.broadcasted_iota(I32, (tl, tl), 0)
    tgt = lax.broadcasted_iota(I32, (tl, tl), 1)
    earlier = jnp.where(src < tgt, 1.0, 0.0).astype(BF16)
    before = jnp.dot(onehot.astype(BF16), earlier, preferred_element_type=F32) + run_ref[...]
    rank = jnp.sum(jnp.where(hit, before, 0.0), axis=0, keepdims=True)
    run_ref[...] = run_ref[...] + jnp.sum(onehot, axis=1, keepdims=True)
    cnt_ref[...] = jnp.broadcast_to(run_ref[...], cnt_ref.shape)

    route = jnp.zeros((SUBLANES, tl), F32)
    for i, v in enumerate((cls.astype(F32), rank, wa, wb)):
        route = jnp.where(row == i, v, route)
    route_ref[0] = route

    diag = src == tgt
    wa_col = jnp.sum(jnp.where(diag, wa, 0.0), axis=1, keepdims=True)
    wb_col = jnp.sum(jnp.where(diag, wb, 0.0), axis=1, keepdims=True)
    lane = lax.broadcasted_iota(I32, (tl, LANES), 1)
    h2_ref[0, :, 0:d] = h2
    h2_ref[0, :, d:d + LANES] = jnp.where(lane == 0, wa_col, jnp.where(lane == 1, wb_col, 0.0))


def _route_rows(n_groups):
    return -(-(SUBLANES * (n_groups + 1)) // 16) * 16


def _tail_specs(b, lt, d, tl, n_ctx_tiles, n_groups):
    nt = lt // tl
    rr = _route_rows(n_groups)
    const = lambda shape: pl.BlockSpec(shape, lambda bi, l: (0,) * len(shape))
    in_specs = [const((1, d)),
                _mod_spec(3, d, n_ctx_tiles),
                _mod_spec(4, d, n_ctx_tiles),
                const((rr, d)), const((rr, d)), const((rr, 1))]
    out_specs = [pl.BlockSpec((1, tl, d), lambda bi, l: (bi, l, 0)),
                 pl.BlockSpec((1, tl, d + LANES), lambda bi, l: (bi, l, 0)),
                 pl.BlockSpec((1, SUBLANES, tl), lambda bi, l: (bi * nt + l, 0, 0)),
                 const((LANES, LANES))]
    out_shape = [jax.ShapeDtypeStruct((b, lt, d), F32),
                 jax.ShapeDtypeStruct((b, lt, d + LANES), F32),
                 jax.ShapeDtypeStruct((b * nt, SUBLANES, tl), F32),
                 jax.ShapeDtypeStruct((LANES, LANES), F32)]
    return in_specs, out_specs, out_shape


def _init_run(run_ref):
    @pl.when((pl.program_id(0) == 0) & (pl.program_id(1) == 0))
    def _():
        run_ref[...] = jnp.zeros_like(run_ref)


def _router_weights(w_group, b_group, w_expert, b_expert):
    d, n_groups = w_group.shape
    n_exp = w_expert.shape[-1]
    rr = _route_rows(n_groups)
    w = jnp.zeros((rr, d), F32).at[0:n_groups].set(w_group.T)
    w = w.at[SUBLANES:SUBLANES * (n_groups + 1)].set(jnp.transpose(w_expert, (0, 2, 1)).reshape(n_groups * n_exp, d))
    bias = jnp.zeros((rr, 1), F32).at[0:n_groups, 0].set(b_group)
    bias = bias.at[SUBLANES:SUBLANES * (n_groups + 1), 0].set(b_expert.reshape(-1))
    w_hi = w.astype(BF16)
    return w_hi, (w - w_hi.astype(F32)).astype(BF16), bias


def _glu_kernel(x_ref, yf_ref, yr_ref, h_ref, dd_ref, w_ref, g1_ref,
                g_ref, sh_ref, sc_ref, wrh_ref, wrl_ref, br_ref,
                xo_ref, h2_ref, route_ref, cnt_ref, run_ref, *, n_groups, n_exp):
    _init_run(run_ref)
    d = x_ref.shape[-1]
    u = yf_ref[...].astype(F32) + yr_ref[...].astype(F32) + dd_ref[...] * h_ref[...].astype(F32)
    z = 0.5 * u * (1.0 + jnp.tanh(math.sqrt(2.0 / math.pi) * (u + 0.044715 * (u * u * u))))
    o = jnp.dot(z.astype(BF16), w_ref[...], preferred_element_type=F32)
    mix = o[:, :d] * _sigmoid(o[:, d:])
    xn = x_ref[0] + g1_ref[0] * mix
    _route_tail(xn, g_ref, sh_ref, sc_ref, wrh_ref, wrl_ref, br_ref, run_ref, xo_ref, h2_ref, route_ref, cnt_ref,
                n_groups=n_groups, n_exp=n_exp)


def _glu_tail(xc, yft, yrt, ht, s5_d, w_glu, modt, g_ffn, router, tl, n_ctx_tiles, n_groups, n_exp):
    b, lt, d = xc.shape
    tm_spec = pl.BlockSpec((tl, d), lambda bi, l: (l, bi))
    t_in, t_out, t_shape = _tail_specs(b, lt, d, tl, n_ctx_tiles, n_groups)
    return pl.pallas_call(
        functools.partial(_glu_kernel, n_groups=n_groups, n_exp=n_exp),
        grid=(b, lt // tl),
        in_specs=[pl.BlockSpec((1, tl, d), lambda bi, l: (bi, l, 0)), tm_spec, tm_spec, tm_spec,
                  pl.BlockSpec((1, d), lambda bi, l: (0, 0)),
                  pl.BlockSpec((d, 2 * d), lambda bi, l: (0, 0)),
                  _mod_spec(2, d, n_ctx_tiles)] + t_in,
        out_specs=t_out,
        out_shape=t_shape,
        scratch_shapes=[pltpu.VMEM((LANES, 1), F32)],
        input_output_aliases={0: 0},
        compiler_params=_cparams(2),
        name="s5_glu_router",
    )(xc, yft, yrt, ht, s5_d.reshape(1, d), w_glu.astype(BF16), modt,
      g_ffn.reshape(1, d), modt, modt, *router)


def _mla_proj_kernel(h_ref, w1_ref, gq_ref, gkv_ref, w2_ref, w2r_ref, wk_ref, wvt_ref, cos_ref, sin_ref,
                     q_ref, k_ref, vt_ref, *, q_rank, kv_rank, n_heads, scale):
    p = jnp.dot(h_ref[0], w1_ref[...], preferred_element_type=F32)
    o_kr = q_rank + kv_rank
    cqn = _rms(p[:, :q_rank], gq_ref[...]).astype(BF16)
    ckvn = _rms(p[:, q_rank:o_kr], gkv_ref[...]).astype(BF16)
    cosp, sinp = cos_ref[...], sin_ref[...]
    kr = (p[:, o_kr:o_kr + LANES] * cosp + p[:, o_kr + LANES:o_kr + 2 * LANES] * sinp).astype(BF16)
    qm = jnp.dot(cqn, w2_ref[...], preferred_element_type=F32)
    qrot = jnp.dot(cqn, w2r_ref[...], preferred_element_type=F32)
    kn = jnp.dot(ckvn, wk_ref[...], preferred_element_type=F32)
    vt = lax.dot_general(wvt_ref[...], ckvn, NT_DIMS, preferred_element_type=F32)
    hn = n_heads * MLA_NOPE
    for h in range(n_heads):
        s = slice(h * LANES, (h + 1) * LANES)
        sr = slice(hn + h * LANES, hn + (h + 1) * LANES)
        q_ref[0, h, :, 0:LANES] = (qm[:, s] * scale).astype(BF16)
        q_ref[0, h, :, LANES:2 * LANES] = ((qm[:, sr] * cosp + qrot[:, s] * sinp) * scale).astype(BF16)
        k_ref[0, h, :, 0:LANES] = kn[:, s].astype(BF16)
        k_ref[0, h, :, LANES:2 * LANES] = kr
        vt_ref[0, h] = vt[h * MLA_V:(h + 1) * MLA_V].astype(BF16)


def _rot_cols(w):
    a1, a2, b1, b2 = jnp.split(w, 4, axis=-1)
    return jnp.concatenate([-a2, a1, -b2, b1], axis=-1)


def _mla_weights(w_in, w_uq, w_ukv, q_rank, kv_rank):
    h = MLA_HEADS
    o_kr = q_rank + kv_rank
    w_kr = w_in[:, o_kr:]
    w1 = jnp.concatenate([w_in[:, :o_kr], _pad_lanes(w_kr), _pad_lanes(_rot_cols(w_kr))], axis=-1)
    uq = w_uq.reshape(q_rank, h, MLA_NOPE + MLA_ROPE)
    uq_n, uq_r = uq[..., :MLA_NOPE], uq[..., MLA_NOPE:]
    w2 = jnp.concatenate([uq_n.reshape(q_rank, h * MLA_NOPE), _pad_lanes(uq_r).reshape(q_rank, h * LANES)], axis=-1)
    w2r = _pad_lanes(_rot_cols(uq_r)).reshape(q_rank, h * LANES)
    ukv = w_ukv.reshape(kv_rank, h, MLA_NOPE + MLA_V)
    wk = ukv[..., :MLA_NOPE].reshape(kv_rank, h * MLA_NOPE)
    wvt = ukv[..., MLA_NOPE:].reshape(kv_rank, h * MLA_V).T
    return [w.astype(BF16) for w in (w1, w2, w2r, wk, wvt)]


def _rope_tables(c_len, seq):
    half = MLA_ROPE // 2
    inv_freq = 1.0 / (ROPE_THETA ** (jnp.arange(0, half, 2, dtype=F32) / half))
    pos = jnp.arange(seq, dtype=I32)
    ang_r = (pos // GRID_W).astype(F32)[:, None] * inv_freq
    ang_c = (pos % GRID_W).astype(F32)[:, None] * inv_freq
    ang = jnp.concatenate([ang_r, ang_r, ang_c, ang_c], axis=-1)
    ang = jnp.concatenate([jnp.zeros((c_len, MLA_ROPE), F32), ang], axis=0)
    return _pad_lanes(jnp.cos(ang)), _pad_lanes(jnp.sin(ang))


def _mla_proj(h, weights, g_q, g_kv, cosp, sinp, tl):
    b, lt, d = h.shape
    w1, w2, w2r, wk, wvt = weights
    q_rank, kv_rank = g_q.shape[0], g_kv.shape[0]
    nh = MLA_HEADS
    full = lambda a: pl.BlockSpec(a.shape, lambda bi, l: (0,) * a.ndim)
    scale = math.log2(math.e) / math.sqrt(MLA_NOPE + MLA_ROPE)
    return pl.pallas_call(
        functools.partial(_mla_proj_kernel, q_rank=q_rank, kv_rank=kv_rank, n_heads=nh, scale=scale),
        grid=(b, lt // tl),
        in_specs=[pl.BlockSpec((1, tl, d), lambda bi, l: (bi, l, 0)), full(w1),
                  pl.BlockSpec((1, q_rank), lambda bi, l: (0, 0)),
                  pl.BlockSpec((1, kv_rank), lambda bi, l: (0, 0)),
                  full(w2), full(w2r), full(wk), full(wvt),
                  pl.BlockSpec((tl, LANES), lambda bi, l: (l, 0)),
                  pl.BlockSpec((tl, LANES), lambda bi, l: (l, 0))],
        out_specs=[pl.BlockSpec((1, nh, tl, 2 * LANES), lambda bi, l: (bi, 0, l, 0)),
                   pl.BlockSpec((1, nh, tl, 2 * LANES), lambda bi, l: (bi, 0, l, 0)),
                   pl.BlockSpec((1, nh, MLA_V, tl), lambda bi, l: (bi, 0, 0, l))],
        out_shape=[jax.ShapeDtypeStruct((b, nh, lt, 2 * LANES), BF16),
                   jax.ShapeDtypeStruct((b, nh, lt, 2 * LANES), BF16),
                   jax.ShapeDtypeStruct((b, nh, MLA_V, lt), BF16)],
        compiler_params=_cparams(2),
        name="mla_proj",
    )(h, w1, g_q.reshape(1, q_rank), g_kv.reshape(1, kv_rank), w2, w2r, wk, wvt, cosp, sinp)


def _attn_kernel(q_ref, k_ref, vt_ref, o_ref, s0_ref, s1_ref, m0_ref, m1_ref, *, n_ctx_tiles, c_len, n_heads, ck):
    def run(lk):
        def scores(h, s_ref, m_ref):
            q = q_ref[0, h]
            m = None
            for c0 in range(0, lk, ck):
                s = lax.dot_general(k_ref[0, h, c0:c0 + ck, :], q, NT_DIMS, preferred_element_type=F32)
                s_ref[c0:c0 + ck, :] = s
                cm = jnp.max(s, axis=0, keepdims=True)
                m = cm if m is None else jnp.maximum(m, cm)
            m_ref[...] = m

        def apply_v(h, s_ref, m_ref):
            p = jnp.exp2(s_ref[0:lk, :] - m_ref[...])
            den = jnp.sum(p, axis=0, keepdims=True)
            acc = jnp.dot(vt_ref[0, h, :, 0:lk], p.astype(BF16), preferred_element_type=F32)
            o_ref[0, h] = (acc / den).T.astype(o_ref.dtype)

        def head_pair(i, carry):
            h0 = 2 * i
            scores(h0 + 1, s1_ref, m1_ref)
            apply_v(h0, s0_ref, m0_ref)
            scores(jnp.minimum(h0 + 2, n_heads - 2), s0_ref, m0_ref)
            apply_v(h0 + 1, s1_ref, m1_ref)
            return carry

        scores(0, s0_ref, m0_ref)
        lax.fori_loop(0, n_heads // 2, head_pair, 0)

    qi = pl.program_id(1)

    @pl.when(qi < n_ctx_tiles)
    def _():
        run(c_len)

    @pl.when(qi >= n_ctx_tiles)
    def _():
        run(k_ref.shape[2])


def _attention(q, k, vt, tl, c_len):
    b, nh, lt, dk = q.shape
    ck = min(ATTN_KEY_CHUNK, c_len)
    return pl.pallas_call(
        functools.partial(_attn_kernel, n_ctx_tiles=c_len // tl, c_len=c_len, n_heads=nh, ck=ck),
        grid=(b, lt // tl),
        in_specs=[pl.BlockSpec((1, nh, tl, dk), lambda bi, l: (bi, 0, l, 0)),
                  pl.BlockSpec((1, nh, lt, dk), lambda bi, l: (bi, 0, 0, 0)),
                  pl.BlockSpec((1, nh, MLA_V, lt), lambda bi, l: (bi, 0, 0, 0))],
        out_specs=pl.BlockSpec((1, nh, tl, MLA_V), lambda bi, l: (bi, 0, l, 0)),
        out_shape=jax.ShapeDtypeStruct((b, nh, lt, MLA_V), BF16),
        scratch_shapes=[pltpu.VMEM((lt, tl), F32), pltpu.VMEM((lt, tl), F32),
                        pltpu.VMEM((1, tl), F32), pltpu.VMEM((1, tl), F32)],
        compiler_params=_cparams(2),
        name="mla_attn",
    )(q, k, vt)


def _oproj_kernel(x_ref, o_ref, w_ref, g1_ref, g_ref, sh_ref, sc_ref, wrh_ref, wrl_ref, br_ref,
                  xo_ref, h2_ref, route_ref, cnt_ref, run_ref, *, n_heads, n_groups, n_exp):
    _init_run(run_ref)
    o = jnp.concatenate([o_ref[0, h] for h in range(n_heads)], axis=-1)
    mix = jnp.dot(o, w_ref[...], preferred_element_type=F32)
    xn = x_ref[0] + g1_ref[0] * mix
    _route_tail(xn, g_ref, sh_ref, sc_ref, wrh_ref, wrl_ref, br_ref, run_ref, xo_ref, h2_ref, route_ref, cnt_ref,
                n_groups=n_groups, n_exp=n_exp)


def _oproj_tail(xc, o, w_o, modt, g_ffn, router, tl, n_ctx_tiles, n_groups, n_exp):
    b, lt, d = xc.shape
    nh = o.shape[1]
    t_in, t_out, t_shape = _tail_specs(b, lt, d, tl, n_ctx_tiles, n_groups)
    return pl.pallas_call(
        functools.partial(_oproj_kernel, n_heads=nh, n_groups=n_groups, n_exp=n_exp),
        grid=(b, lt // tl),
        in_specs=[pl.BlockSpec((1, tl, d), lambda bi, l: (bi, l, 0)),
                  pl.BlockSpec((1, nh, tl, MLA_V), lambda bi, l: (bi, 0, l, 0)),
                  pl.BlockSpec(w_o.shape, lambda bi, l: (0, 0)),
                  _mod_spec(2, d, n_ctx_tiles)] + t_in,
        out_specs=t_out,
        out_shape=t_shape,
        scratch_shapes=[pltpu.VMEM((LANES, 1), F32)],
        input_output_aliases={0: 0},
        compiler_params=_cparams(2),
        name="mla_out_router",
    )(xc, o, w_o.astype(BF16), modt, g_ffn.reshape(1, d), modt, modt, *router)


def _expert_kernel(ea_ref, eb_ref, nu_ref, src_ref, src_next_ref, dst_ref, h_ref,
                   wgu_a_ref, wd_a_ref, wgu_b_ref, wd_b_ref, y_ref, xbuf, ybuf, gsem, ssem, *, tm, d, n_tok):
    del ea_ref, eb_ref
    j = pl.program_id(0)
    n_used = nu_ref[0]
    slot = j % 2
    other = 1 - slot

    def gather_row(idx, s, r):
        return pltpu.make_async_copy(h_ref.at[pl.ds(idx, 1)], xbuf.at[s, pl.ds(r, 1)], gsem.at[s])

    def scatter_row(idx, s, r):
        return pltpu.make_async_copy(ybuf.at[s, pl.ds(r, 1)], y_ref.at[pl.ds(idx, 1)], ssem.at[s])

    def wait_gather(s):
        for r in range(tm):
            gather_row(0, s, r).wait()

    def wait_scatter(s):
        for r in range(tm):
            scatter_row(0, s, r).wait()

    @pl.when(j == 0)
    def _():
        ybuf[...] = jnp.zeros_like(ybuf)
        for r in range(tm):
            scatter_row(n_tok + r, 0, r).start(priority=r % 2)
            gather_row(src_ref[0, 0, r], 0, r).start(priority=r % 2)

    @pl.when(j <= n_used)
    def _():
        wait_gather(slot)
        wait_scatter(slot)
        xw = xbuf[slot]
        x = xw[:, 0:d].astype(BF16)
        for r in range(tm):
            scatter_row(dst_ref[0, 0, r], other, r).start(priority=r % 2)
            gather_row(src_next_ref[0, 0, r], other, r).start(priority=r % 2)
        hid = wd_a_ref.shape[1]

        def ffn(wgu_ref, wd_ref):
            gu = jnp.dot(x, wgu_ref[0], preferred_element_type=F32)
            gate, up = gu[:, :hid], gu[:, hid:]
            act = (gate * _sigmoid(gate) * up).astype(BF16)
            return jnp.dot(act, wd_ref[0], preferred_element_type=F32)

        ybuf[slot] = xw[:, d:d + 1] * ffn(wgu_a_ref, wd_a_ref) + xw[:, d + 1:d + 2] * ffn(wgu_b_ref, wd_b_ref)

        @pl.when(j == n_used)
        def _():
            wait_scatter(other)
            wait_gather(other)


def _pair_tables(n_groups, n_exp):
    a, b = np.triu_indices(n_exp, k=1)
    base = (np.arange(n_groups) * n_exp)[:, None]
    return (base + a[None, :]).reshape(-1).astype(np.int32), (base + b[None, :]).reshape(-1).astype(np.int32)


def _moe(h2, route, counts, wgu, wd, tl, tm, n_groups, n_exp):
    b, lt, dw = h2.shape
    d = dw - LANES
    t = b * lt
    cls_a, cls_b = _pair_tables(n_groups, n_exp)
    n_cls = cls_a.shape[0]
    n_tiles = t // tm + n_cls
    cls = route[:, 0, :].reshape(t).astype(I32)
    rank = route[:, 1, :].reshape(t).astype(I32)
    cnt = counts[:n_cls, 0].astype(I32)
    padded = ((cnt + tm - 1) // tm) * tm
    seg_end = jnp.cumsum(padded)
    seg_start = seg_end - padded
    pos = seg_start[cls] + rank
    rows = jnp.arange(n_tiles * tm, dtype=I32)
    spare = t + ((rows // tm) % 2) * tm + rows % tm
    dst = spare.at[pos].set(jnp.arange(t, dtype=I32))
    src = jnp.where(dst < t, dst, 0)
    n_used = (seg_end[-1] // tm).astype(I32)
    tile = jnp.arange(n_tiles + 1, dtype=I32)
    tile_cls = jnp.sum((seg_end[None, :] <= (tile * tm)[:, None]).astype(I32), axis=1)
    tile_cls = jnp.minimum(jnp.where(tile < n_used, tile_cls, tile_cls[n_used - 1]), n_cls - 1)
    ea, eb = jnp.asarray(cls_a)[tile_cls], jnp.asarray(cls_b)[tile_cls]
    dst_prev = jnp.concatenate([t + tm + jnp.arange(tm, dtype=I32), dst]).reshape(n_tiles + 1, 1, tm)
    src = src.reshape(n_tiles, 1, tm)

    two_h = wgu.shape[-1]
    idx_blk = (1, 1, tm)
    smem = lambda f: pl.BlockSpec(idx_blk, f, memory_space=pltpu.SMEM)
    wspec = lambda shape, which: pl.BlockSpec(shape, lambda j, a, bb, nu: ((a, bb)[which][j], 0, 0))
    return pl.pallas_call(
        functools.partial(_expert_kernel, tm=tm, d=d, n_tok=t),
        grid_spec=pltpu.PrefetchScalarGridSpec(
            num_scalar_prefetch=3,
            grid=(n_tiles + 1,),
            in_specs=[smem(lambda j, a, bb, nu: (0, 0, 0)),
                      smem(lambda j, a, bb, nu: (jnp.minimum(j + 1, n_tiles - 1), 0, 0)),
                      smem(lambda j, a, bb, nu: (j, 0, 0)),
                      pl.BlockSpec(memory_space=pl.ANY),
                      wspec((1, d, two_h), 0), wspec((1, two_h // 2, d), 0),
                      wspec((1, d, two_h), 1), wspec((1, two_h // 2, d), 1)],
            out_specs=pl.BlockSpec(memory_space=pl.ANY),
            scratch_shapes=[pltpu.VMEM((2, tm, dw), F32), pltpu.VMEM((2, tm, d), F32),
                            pltpu.SemaphoreType.DMA((2,)), pltpu.SemaphoreType.DMA((2,))]),
        out_shape=jax.ShapeDtypeStruct((t + 2 * tm, d), F32),
        compiler_params=_cparams(1),
        name="moe_experts",
    )(ea, eb, n_used.reshape(1), src, src, dst_prev, h2.reshape(t, dw), wgu, wd, wgu, wd)


def kernel(x, c, ctx, c_ctx, ada_w, ada_b, norm_mix_g, norm_ffn_g, s5_a_re, s5_a_im, s5_log_dt, s5_b_re, s5_b_im, s5_c_re, s5_c_im, s5_d, s5_w_glu, mla_w_in, mla_g_q, mla_g_kv, mla_w_uq, mla_w_ukv, mla_w_o, moe_w_group, moe_b_group, moe_w_expert, moe_b_expert, moe_w_gate_up, moe_w_down, final_g):
    b, seq, d = x.shape
    c_len = ctx.shape[1]
    depth = ada_w.shape[0]
    n_groups, n_exp = moe_w_group.shape[-1], moe_w_expert.shape[-1]
    assert n_exp == SUBLANES and n_groups < SUBLANES and n_groups * n_exp * (n_exp - 1) // 2 <= LANES
    assert seq % GRID_W == 0
    tl = _token_tile(c_len, seq)
    n_ctx_tiles = c_len // tl
    tm = min(MOE_TILE, tl)
    tc = min(64, tl)
    gpb = MXU_DIM // s5_b_re.shape[-1]

    xc = jnp.concatenate([ctx, x], axis=1)
    bp = -(-(b + 1) // SUBLANES) * SUBLANES
    cc = jnp.concatenate([c, c_ctx[None, :], jnp.zeros((bp - b - 1, d), F32)], axis=0)
    mod = _ada(cc, ada_w, ada_b)
    cosp, sinp = _rope_tables(c_len, seq)
    wgu_all, wd_all = moe_w_gate_up.astype(BF16), moe_w_down.astype(BF16)

    moe = None
    for i in range(depth):
        j = i // N_MIXERS
        mod_c = jnp.broadcast_to(mod[i, b][None, :], (b, ADA_CHUNKS * d))
        modt = jnp.stack([mod_c, mod[i, :b]], axis=1).reshape(2 * b, 1, ADA_CHUNKS * d)
        router = _router_weights(moe_w_group[i], moe_b_group[i], moe_w_expert[i], moe_b_expert[i])
        if i % N_MIXERS == 0:
            xc, ht = _norm_mod(xc, norm_mix_g[i], modt, tl, n_ctx_tiles, True, moe)
            abr, abi, bbr, bbi = _s5_discretize(s5_a_re[j], s5_a_im[j], s5_log_dt[j], s5_b_re[j], s5_b_im[j])
            bm, cm, a = _s5_block_mats(abr, abi, bbr, bbi, s5_c_re[j], s5_c_im[j], gpb)
            yf, yr = _s5_scan(ht.reshape(c_len + seq, b, d), bm, cm, a, c_len, tc)
            flat = lambda y: y.reshape(c_len + seq, b * d)
            xc, h2, route, counts = _glu_tail(xc, flat(yf), flat(yr), ht, s5_d[j], s5_w_glu[j], modt,
                                              norm_ffn_g[i], router, tl, n_ctx_tiles, n_groups, n_exp)
        else:
            xc, h = _norm_mod(xc, norm_mix_g[i], modt, tl, n_ctx_tiles, False, moe)
            weights = _mla_weights(mla_w_in[j], mla_w_uq[j], mla_w_ukv[j], mla_g_q.shape[-1], mla_g_kv.shape[-1])
            q, k, vt = _mla_proj(h, weights, mla_g_q[j], mla_g_kv[j], cosp, sinp, tl)
            o = _attention(q, k, vt, tl, c_len)
            xc, h2, route, counts = _oproj_tail(xc, o, mla_w_o[j], modt, norm_ffn_g[i], router,
                                                tl, n_ctx_tiles, n_groups, n_exp)
        moe = (_moe(h2, route, counts, wgu_all[i], wd_all[i], tl, tm, n_groups, n_exp), modt)
    return _final_norm(xc, moe[0], moe[1], final_g, tl, n_ctx_tiles, seq)
```

```python
import functools
import math

import jax
import jax.numpy as jnp
import numpy as np
from jax import lax
from jax.experimental import pallas as pl
from jax.experimental.pallas import tpu as pltpu

F32 = jnp.float32
BF16 = jnp.bfloat16
I32 = jnp.int32
U32 = jnp.uint32

EPS = 1e-6
GRID_W = 64
ROPE_THETA = 10000.0
ADA_CHUNKS = 6
N_MIXERS = 2
MLA_HEADS = 8
MLA_NOPE = 128
MLA_ROPE = 64
MLA_V = 128
MOE_TOP_K = 2

LANES = 128
SUBLANES = 8
MXU_DIM = 256
VMEM_LIMIT = 56 * 1024 * 1024
NEG_BIG = -1e30
NT_DIMS = (((1,), (1,)), ((), ()))
HIGH_HALF = np.uint32(0xFFFF0000)
MOE_TILE = 128
ATTN_KEY_CHUNK = 256


def _cparams(n_axes, vmem=VMEM_LIMIT):
    return pltpu.CompilerParams(dimension_semantics=("arbitrary",) * n_axes, vmem_limit_bytes=vmem)


def _sigmoid(x):
    return 1.0 / (1.0 + jnp.exp(-x))


def _rms(x, g):
    return x * lax.rsqrt(jnp.mean(x * x, axis=-1, keepdims=True) + EPS) * g


def _token_tile(c_len, seq):
    t = 256
    while c_len % t or seq % t:
        t //= 2
    return t


def _pad_lanes(w):
    return jnp.concatenate([w, jnp.zeros(w.shape[:-1] + (LANES - w.shape[-1],), w.dtype)], axis=-1)


def _ada_kernel(cc_ref, w_ref, b_ref, o_ref):
    cc = cc_ref[...]
    s = cc * _sigmoid(cc)
    o_ref[0] = jnp.dot(s, w_ref[0], preferred_element_type=F32, precision=lax.Precision.HIGHEST) + b_ref[0]


def _ada(cc, ada_w, ada_b):
    depth, d, n = ada_w.shape
    bp = cc.shape[0]
    tn = n // 4
    return pl.pallas_call(
        _ada_kernel,
        grid=(depth, n // tn),
        in_specs=[pl.BlockSpec((bp, d), lambda i, j: (0, 0)),
                  pl.BlockSpec((1, d, tn), lambda i, j: (i, 0, j)),
                  pl.BlockSpec((1, 1, tn), lambda i, j: (i, 0, j))],
        out_specs=pl.BlockSpec((1, bp, tn), lambda i, j: (i, 0, j)),
        out_shape=jax.ShapeDtypeStruct((depth, bp, n), F32),
        compiler_params=_cparams(2),
        name="ada_mod",
    )(cc, ada_w, ada_b.reshape(depth, 1, n))


def _mod_spec(chunk, d, n_ctx_tiles, shift=0):
    return pl.BlockSpec((1, 1, d), lambda b, l: (2 * b + jnp.where(l + shift >= n_ctx_tiles, 1, 0), 0, chunk))


def _norm_mod_kernel(x_ref, g_ref, sh_ref, sc_ref, o_ref):
    y = _rms(x_ref[0], g_ref[...])
    o_ref[...] = (y * (1.0 + sc_ref[0]) + sh_ref[0]).astype(o_ref.dtype).reshape(o_ref.shape)


def _token_rows(y_ref, tl):
    return jnp.concatenate([y_ref[pl.ds(k, tl, stride=SUBLANES), :] for k in range(SUBLANES)], axis=-1)


def _res_norm_mod_kernel(x_ref, y_ref, g2_ref, g_ref, sh_ref, sc_ref, xo_ref, o_ref):
    xn = x_ref[0] + g2_ref[0] * _token_rows(y_ref, x_ref.shape[1])
    xo_ref[0] = xn
    o_ref[...] = (_rms(xn, g_ref[...]) * (1.0 + sc_ref[0]) + sh_ref[0]).astype(o_ref.dtype).reshape(o_ref.shape)


def _norm_mod(xc, g, modt, tl, n_ctx_tiles, time_major, moe=None):
    b, lt, d = xc.shape
    nt = lt // tl
    tok = pl.BlockSpec((1, tl, d), lambda bi, l: (bi, l, 0))
    if time_major:
        h_shape = jax.ShapeDtypeStruct((lt, b * d), BF16)
        h_spec = pl.BlockSpec((tl, d), lambda bi, l: (l, bi))
    else:
        h_shape, h_spec = jax.ShapeDtypeStruct((b, lt, d), BF16), tok
    tail_specs = [pl.BlockSpec((1, d), lambda bi, l: (0, 0)),
                  _mod_spec(0, d, n_ctx_tiles), _mod_spec(1, d, n_ctx_tiles)]
    if moe is None:
        h = pl.pallas_call(
            _norm_mod_kernel, grid=(b, nt), in_specs=[tok] + tail_specs, out_specs=h_spec, out_shape=h_shape,
            compiler_params=_cparams(2), name="norm_mod",
        )(xc, g.reshape(1, d), modt, modt)
        return xc, h
    y, prev_modt = moe
    return pl.pallas_call(
        _res_norm_mod_kernel, grid=(b, nt),
        in_specs=[tok, pl.BlockSpec((SUBLANES * tl, LANES), lambda bi, l: (bi * nt + l, 0)), _mod_spec(5, d, n_ctx_tiles)] + tail_specs,
        out_specs=[tok, h_spec],
        out_shape=[jax.ShapeDtypeStruct((b, lt, d), F32), h_shape],
        input_output_aliases={0: 0},
        compiler_params=_cparams(2), name="moe_res_norm_mod",
    )(xc, y, prev_modt, g.reshape(1, d), modt, modt)


def _final_norm_kernel(x_ref, y_ref, g2_ref, g_ref, o_ref):
    o_ref[0] = _rms(x_ref[0] + g2_ref[0] * _token_rows(y_ref, x_ref.shape[1]), g_ref[...])


def _final_norm(xc, y, modt, g, tl, n_ctx_tiles, seq):
    b, lt, d = xc.shape
    nt = lt // tl
    return pl.pallas_call(
        _final_norm_kernel,
        grid=(b, seq // tl),
        in_specs=[pl.BlockSpec((1, tl, d), lambda bi, l: (bi, l + n_ctx_tiles, 0)),
                  pl.BlockSpec((SUBLANES * tl, LANES), lambda bi, l: (bi * nt + l + n_ctx_tiles, 0)),
                  _mod_spec(5, d, n_ctx_tiles, shift=n_ctx_tiles),
                  pl.BlockSpec((1, d), lambda bi, l: (0, 0))],
        out_specs=pl.BlockSpec((1, tl, d), lambda bi, l: (bi, l, 0)),
        out_shape=jax.ShapeDtypeStruct((b, seq, d), F32),
        compiler_params=_cparams(2),
        name="final_norm",
    )(xc, y, modt, g.reshape(1, d))


def _s5_disc_kernel(ar_ref, ai_ref, ldt_ref, br_ref, bi_ref, abr_ref, abi_ref, bbr_ref, bbi_ref):
    ar, ai = ar_ref[...], ai_ref[...]
    dt = jnp.exp(ldt_ref[...])
    ldr, ldi = ar * dt, ai * dt
    mag = jnp.exp(ldr)
    abr, abi = mag * jnp.cos(ldi), mag * jnp.sin(ldi)
    nr, ni = abr - 1.0, abi
    den = ar * ar + ai * ai
    qr = (nr * ar + ni * ai) / den
    qi = (ni * ar - nr * ai) / den
    br, bi = br_ref[...], bi_ref[...]
    abr_ref[...] = abr
    abi_ref[...] = abi
    bbr_ref[...] = qr * br - qi * bi
    bbi_ref[...] = qr * bi + qi * br


def _s5_discretize(a_re, a_im, log_dt, b_re, b_im):
    two, g, p, c = b_re.shape
    rows, cols = two * g, p * c
    wide = lambda a: jnp.broadcast_to(a[..., None], (two, g, p, c)).reshape(rows, cols)
    ldt = jnp.broadcast_to(log_dt[:, :, None, None], (two, g, p, c)).reshape(rows, cols)
    spec = pl.BlockSpec((rows, cols), lambda: (0, 0))
    shape = jax.ShapeDtypeStruct((rows, cols), F32)
    abr, abi, bbr, bbi = pl.pallas_call(
        _s5_disc_kernel,
        in_specs=[spec] * 5,
        out_specs=[spec] * 4,
        out_shape=[shape] * 4,
        name="s5_disc",
    )(wide(a_re), wide(a_im), ldt, b_re.reshape(rows, cols), b_im.reshape(rows, cols))
    pick = lambda a: a.reshape(two, g, p, c)[..., 0]
    return pick(abr), pick(abi), bbr.reshape(two, g, p, c), bbi.reshape(two, g, p, c)


def _s5_block_mats(abr, abi, bbr, bbi, c_re, c_im, gpb):
    two, g, p, c = bbr.shape
    ncb = g // gpb
    eye = jnp.eye(gpb, dtype=F32)

    def in_mat(bb):
        bb = bb.reshape(two, ncb, gpb, p, c)
        return jnp.einsum('dngpc,gh->dngchp', bb, eye).reshape(two, ncb, gpb * c, gpb * p)

    def out_mat(cc):
        cc = cc.reshape(two, ncb, gpb, c, p)
        return jnp.einsum('dngcp,gh->dngphc', cc, eye).reshape(two, ncb, gpb * p, gpb * c)

    bm = jnp.concatenate([in_mat(bbr), in_mat(bbi)], axis=-1).astype(BF16)
    cm = jnp.concatenate([out_mat(c_re), -out_mat(c_im)], axis=-2).astype(BF16)
    a = jnp.stack([abr.reshape(two, ncb, gpb * p), abi.reshape(two, ncb, gpb * p)], axis=2)
    return bm, cm, a


def _s5_kernel(xf_ref, xr_ref, bm_ref, cm_ref, a_ref, yf_ref, yr_ref, buf_ref, bur_ref, st_ref,
               *, tc, nb, cw, ns, rb, lw):
    @pl.when(pl.program_id(1) == 0)
    def _():
        st_ref[...] = jnp.zeros_like(st_ref)

    tpb = rb // nb
    nrb = tc // tpb
    refs = ((xf_ref, yf_ref, buf_ref), (xr_ref, yr_ref, bur_ref))
    order = (tuple(range(nrb)), tuple(range(nrb - 1, -1, -1)))

    def inject(d, r):
        x_ref, _, bu_ref = refs[d]
        x = x_ref[r * tpb:(r + 1) * tpb].reshape(rb, cw)
        bu_ref[r * rb:(r + 1) * rb, :] = jnp.dot(x, bm_ref[d, 0], preferred_element_type=F32)

    def recur(d, r):
        bu_ref = refs[d][2]
        for lc in range(ns // lw):
            lo = lc * lw
            are = jnp.broadcast_to(a_ref[d, 0, 0:1, lo:lo + lw], (nb, lw))
            aim = jnp.broadcast_to(a_ref[d, 0, 1:2, lo:lo + lw], (nb, lw))
            hr, hi = st_ref[d, 0, :, lo:lo + lw], st_ref[d, 1, :, lo:lo + lw]
            for i in range(tpb):
                t = r * tpb + (i if d == 0 else tpb - 1 - i)
                rows = slice(t * nb, (t + 1) * nb)
                hr, hi = (are * hr - aim * hi + bu_ref[rows, lo:lo + lw],
                          are * hi + aim * hr + bu_ref[rows, ns + lo:ns + lo + lw])
                bu_ref[rows, lo:lo + lw] = hr
                bu_ref[rows, ns + lo:ns + lo + lw] = hi
            st_ref[d, 0, :, lo:lo + lw] = hr
            st_ref[d, 1, :, lo:lo + lw] = hi

    def readout(d, r):
        _, y_ref, bu_ref = refs[d]
        h = bu_ref[r * rb:(r + 1) * rb, :].astype(BF16)
        y = jnp.dot(h, cm_ref[d, 0], preferred_element_type=F32)
        y_ref[r * tpb:(r + 1) * tpb] = y.reshape(tpb, nb, cw).astype(y_ref.dtype)

    for d in range(2):
        inject(d, order[d][0])
    for k in range(nrb):
        for d in range(2):
            if k + 1 < nrb:
                inject(d, order[d][k + 1])
        for d in range(2):
            recur(d, order[d][k])
        for d in range(2):
            if k >= 1:
                readout(d, order[d][k - 1])
    for d in range(2):
        readout(d, order[d][nrb - 1])


def _s5_scan(ht3, bm, cm, a, c_len, tc):
    lt, nb, d = ht3.shape
    _, ncb, cw, ns2 = bm.shape
    ns = ns2 // 2
    n_c, n_all = c_len // tc, lt // tc
    rb = min(256, tc * nb)

    def rev(k):
        return jnp.where(k < n_c, n_c - 1 - k, n_all - 1 - (k - n_c))

    x_blk = (tc, nb, cw)
    kern = functools.partial(_s5_kernel, tc=tc, nb=nb, cw=cw, ns=ns, rb=rb, lw=min(256, ns))
    return pl.pallas_call(
        kern,
        grid=(ncb, n_all),
        in_specs=[pl.BlockSpec(x_blk, lambda j, k: (k, 0, j)),
                  pl.BlockSpec(x_blk, lambda j, k: (rev(k), 0, j)),
                  pl.BlockSpec((2, 1, cw, ns2), lambda j, k: (0, j, 0, 0)),
                  pl.BlockSpec((2, 1, ns2, cw), lambda j, k: (0, j, 0, 0)),
                  pl.BlockSpec((2, 1, 2, ns), lambda j, k: (0, j, 0, 0))],
        out_specs=[pl.BlockSpec(x_blk, lambda j, k: (k, 0, j)),
                   pl.BlockSpec(x_blk, lambda j, k: (rev(k), 0, j))],
        out_shape=[jax.ShapeDtypeStruct((lt, nb, d), BF16)] * 2,
        scratch_shapes=[pltpu.VMEM((tc * nb, ns2), F32), pltpu.VMEM((tc * nb, ns2), F32),
                        pltpu.VMEM((2, 2, nb, ns), F32)],
        compiler_params=_cparams(2),
        name="s5_scan",
    )(ht3, ht3, bm, cm, a)


def _route_tail(xn, g_ref, sh_ref, sc_ref, wrh_ref, wrl_ref, br_ref, run_ref, xo_ref, h2_ref, route_ref, cnt_ref,
                *, n_groups, n_exp):
    tl, d = xn.shape
    xo_ref[0] = xn
    h2 = _rms(xn, g_ref[...]) * (1.0 + sc_ref[0]) + sh_ref[0]
    h_hi = h2.astype(BF16)
    h_lo = (h2 - h_hi.astype(F32)).astype(BF16)
    dg = lambda w, x: lax.dot_general(w, x, NT_DIMS, preferred_element_type=F32)
    logits = dg(wrh_ref[...], h_hi) + (dg(wrh_ref[...], h_lo) + dg(wrl_ref[...], h_hi)) + br_ref[...]

    row = lax.broadcasted_iota(I32, (SUBLANES, tl), 0)

    def softmax0(v, mask):
        v = jnp.where(mask, v, NEG_BIG)
        e = jnp.where(mask, jnp.exp(v - jnp.max(v, axis=0, keepdims=True)), 0.0)
        return e / jnp.sum(e, axis=0, keepdims=True)

    def top1(p, mask):
        v = jnp.max(jnp.where(mask, p, -1.0), axis=0, keepdims=True)
        idx = jnp.min(jnp.where(mask & (p == v), row, SUBLANES), axis=0, keepdims=True)
        return v, idx

    is_g = row < n_groups
    p_g, g_idx = top1(softmax0(logits[0:SUBLANES], is_g), is_g)
    le = logits[SUBLANES:2 * SUBLANES]
    for g in range(1, n_groups):
        le = jnp.where(g_idx == g, logits[(g + 1) * SUBLANES:(g + 2) * SUBLANES], le)
    every = row >= 0
    pe = softmax0(le, every)
    p0, i0 = top1(pe, every)
    p1, i1 = top1(pe, row != i0)
    den = p0 + p1
    w0, w1 = p_g * (p0 / den), p_g * (p1 / den)
    first_low = i0 < i1
    ea, eb = jnp.minimum(i0, i1), jnp.maximum(i0, i1)
    wa, wb = jnp.where(first_low, w0, w1), jnp.where(first_low, w1, w0)
    n_pairs = n_exp * (n_exp - 1) // 2
    pair = ea * (n_exp - 1) - jnp.right_shift(ea * (ea - 1), 1) + (eb - ea - 1)
    cls = g_idx * n_pairs + pair

    crow = lax.broadcasted_iota(I32, (LANES, tl), 0)
    hit = crow == cls
    onehot = jnp.where(hit, 1.0, 0.0)
    src = lax.broadcasted_iota(I32, (tl, tl), 0)
    tgt = lax.broadcasted_iota(I32, (tl, tl), 1)
    earlier = jnp.where(src < tgt, 1.0, 0.0).astype(BF16)
    before = jnp.dot(onehot.astype(BF16), earlier, preferred_element_type=F32) + run_ref[...]
    rank = jnp.sum(jnp.where(hit, before, 0.0), axis=0, keepdims=True)
    run_ref[...] = run_ref[...] + jnp.sum(onehot, axis=1, keepdims=True)
    cnt_ref[...] = jnp.broadcast_to(run_ref[...], cnt_ref.shape)

    route = jnp.zeros((SUBLANES, tl), F32)
    for i, v in enumerate((cls.astype(F32), rank, wa, wb)):
        route = jnp.where(row == i, v, route)
    route_ref[0] = route

    diag = src == tgt
    wa_col = jnp.sum(jnp.where(diag, wa, 0.0), axis=1, keepdims=True)
    wb_col = jnp.sum(jnp.where(diag, wb, 0.0), axis=1, keepdims=True)
    lane = lax.broadcasted_iota(I32, (tl, LANES), 1)
    wrow = jnp.where(lane == 0, wa_col, jnp.where(lane == 1, wb_col, 0.0))
    bits = lax.bitcast_convert_type(h_hi.astype(F32), U32)
    half = d // 2
    packed = (bits[:, 0:half] >> 16) | (bits[:, half:d] & HIGH_HALF)
    n_packed = half // LANES
    for c in range(n_packed):
        h2_ref[pl.ds(c, tl, stride=SUBLANES), :] = packed[:, c * LANES:(c + 1) * LANES]
    h2_ref[pl.ds(n_packed, tl, stride=SUBLANES), :] = lax.bitcast_convert_type(wrow, U32)
    for c in range(n_packed + 1, SUBLANES):
        h2_ref[pl.ds(c, tl, stride=SUBLANES), :] = jnp.zeros((tl, LANES), U32)


def _route_rows(n_groups):
    return -(-(SUBLANES * (n_groups + 1)) // 16) * 16


def _tail_specs(b, lt, d, tl, n_ctx_tiles, n_groups):
    nt = lt // tl
    rr = _route_rows(n_groups)
    const = lambda shape: pl.BlockSpec(shape, lambda bi, l: (0,) * len(shape))
    in_specs = [const((1, d)),
                _mod_spec(3, d, n_ctx_tiles),
                _mod_spec(4, d, n_ctx_tiles),
                const((rr, d)), const((rr, d)), const((rr, 1))]
    out_specs = [pl.BlockSpec((1, tl, d), lambda bi, l: (bi, l, 0)),
                 pl.BlockSpec((SUBLANES * tl, LANES), lambda bi, l: (bi * nt + l, 0)),
                 pl.BlockSpec((1, SUBLANES, tl), lambda bi, l: (bi * nt + l, 0, 0)),
                 const((LANES, LANES))]
    out_shape = [jax.ShapeDtypeStruct((b, lt, d), F32),
                 jax.ShapeDtypeStruct((SUBLANES * b * lt, LANES), U32),
                 jax.ShapeDtypeStruct((b * nt, SUBLANES, tl), F32),
                 jax.ShapeDtypeStruct((LANES, LANES), F32)]
    return in_specs, out_specs, out_shape


def _init_run(run_ref):
    @pl.when((pl.program_id(0) == 0) & (pl.program_id(1) == 0))
    def _():
        run_ref[...] = jnp.zeros_like(run_ref)


def _router_weights(w_group, b_group, w_expert, b_expert):
    d, n_groups = w_group.shape
    n_exp = w_expert.shape[-1]
    rr = _route_rows(n_groups)
    w = jnp.zeros((rr, d), F32).at[0:n_groups].set(w_group.T)
    w = w.at[SUBLANES:SUBLANES * (n_groups + 1)].set(jnp.transpose(w_expert, (0, 2, 1)).reshape(n_groups * n_exp, d))
    bias = jnp.zeros((rr, 1), F32).at[0:n_groups, 0].set(b_group)
    bias = bias.at[SUBLANES:SUBLANES * (n_groups + 1), 0].set(b_expert.reshape(-1))
    w_hi = w.astype(BF16)
    return w_hi, (w - w_hi.astype(F32)).astype(BF16), bias


def _glu_kernel(x_ref, yf_ref, yr_ref, h_ref, dd_ref, w_ref, g1_ref,
                g_ref, sh_ref, sc_ref, wrh_ref, wrl_ref, br_ref,
                xo_ref, h2_ref, route_ref, cnt_ref, run_ref, *, n_groups, n_exp):
    _init_run(run_ref)
    d = x_ref.shape[-1]
    u = yf_ref[...].astype(F32) + yr_ref[...].astype(F32) + dd_ref[...] * h_ref[...].astype(F32)
    z = 0.5 * u * (1.0 + jnp.tanh(math.sqrt(2.0 / math.pi) * (u + 0.044715 * (u * u * u))))
    o = jnp.dot(z.astype(BF16), w_ref[...], preferred_element_type=F32)
    mix = o[:, :d] * _sigmoid(o[:, d:])
    xn = x_ref[0] + g1_ref[0] * mix
    _route_tail(xn, g_ref, sh_ref, sc_ref, wrh_ref, wrl_ref, br_ref, run_ref, xo_ref, h2_ref, route_ref, cnt_ref,
                n_groups=n_groups, n_exp=n_exp)


def _glu_tail(xc, yft, yrt, ht, s5_d, w_glu, modt, g_ffn, router, tl, n_ctx_tiles, n_groups, n_exp):
    b, lt, d = xc.shape
    tm_spec = pl.BlockSpec((tl, d), lambda bi, l: (l, bi))
    t_in, t_out, t_shape = _tail_specs(b, lt, d, tl, n_ctx_tiles, n_groups)
    return pl.pallas_call(
        functools.partial(_glu_kernel, n_groups=n_groups, n_exp=n_exp),
        grid=(b, lt // tl),
        in_specs=[pl.BlockSpec((1, tl, d), lambda bi, l: (bi, l, 0)), tm_spec, tm_spec, tm_spec,
                  pl.BlockSpec((1, d), lambda bi, l: (0, 0)),
                  pl.BlockSpec((d, 2 * d), lambda bi, l: (0, 0)),
                  _mod_spec(2, d, n_ctx_tiles)] + t_in,
        out_specs=t_out,
        out_shape=t_shape,
        scratch_shapes=[pltpu.VMEM((LANES, 1), F32)],
        input_output_aliases={0: 0},
        compiler_params=_cparams(2),
        name="s5_glu_router",
    )(xc, yft, yrt, ht, s5_d.reshape(1, d), w_glu.astype(BF16), modt,
      g_ffn.reshape(1, d), modt, modt, *router)


def _mla_proj_kernel(h_ref, w1_ref, gq_ref, gkv_ref, w2_ref, w2r_ref, wk_ref, wvt_ref, cos_ref, sin_ref,
                     q_ref, k_ref, vt_ref, *, q_rank, kv_rank, n_heads, scale):
    p = jnp.dot(h_ref[0], w1_ref[...], preferred_element_type=F32)
    o_kr = q_rank + kv_rank
    cqn = _rms(p[:, :q_rank], gq_ref[...]).astype(BF16)
    ckvn = _rms(p[:, q_rank:o_kr], gkv_ref[...]).astype(BF16)
    cosp, sinp = cos_ref[...], sin_ref[...]
    kr = (p[:, o_kr:o_kr + LANES] * cosp + p[:, o_kr + LANES:o_kr + 2 * LANES] * sinp).astype(BF16)
    qm = jnp.dot(cqn, w2_ref[...], preferred_element_type=F32)
    qrot = jnp.dot(cqn, w2r_ref[...], preferred_element_type=F32)
    kn = jnp.dot(ckvn, wk_ref[...], preferred_element_type=F32)
    vt = lax.dot_general(wvt_ref[...], ckvn, NT_DIMS, preferred_element_type=F32)
    hn = n_heads * MLA_NOPE
    for h in range(n_heads):
        s = slice(h * LANES, (h + 1) * LANES)
        sr = slice(hn + h * LANES, hn + (h + 1) * LANES)
        q_ref[0, h, :, 0:LANES] = (qm[:, s] * scale).astype(BF16)
        q_ref[0, h, :, LANES:2 * LANES] = ((qm[:, sr] * cosp + qrot[:, s] * sinp) * scale).astype(BF16)
        k_ref[0, h, :, 0:LANES] = kn[:, s].astype(BF16)
        k_ref[0, h, :, LANES:2 * LANES] = kr
        vt_ref[0, h] = vt[h * MLA_V:(h + 1) * MLA_V].astype(BF16)


def _rot_cols(w):
    a1, a2, b1, b2 = jnp.split(w, 4, axis=-1)
    return jnp.concatenate([-a2, a1, -b2, b1], axis=-1)


def _mla_weights(w_in, w_uq, w_ukv, q_rank, kv_rank):
    h = MLA_HEADS
    o_kr = q_rank + kv_rank
    w_kr = w_in[:, o_kr:]
    w1 = jnp.concatenate([w_in[:, :o_kr], _pad_lanes(w_kr), _pad_lanes(_rot_cols(w_kr))], axis=-1)
    uq = w_uq.reshape(q_rank, h, MLA_NOPE + MLA_ROPE)
    uq_n, uq_r = uq[..., :MLA_NOPE], uq[..., MLA_NOPE:]
    w2 = jnp.concatenate([uq_n.reshape(q_rank, h * MLA_NOPE), _pad_lanes(uq_r).reshape(q_rank, h * LANES)], axis=-1)
    w2r = _pad_lanes(_rot_cols(uq_r)).reshape(q_rank, h * LANES)
    ukv = w_ukv.reshape(kv_rank, h, MLA_NOPE + MLA_V)
    wk = ukv[..., :MLA_NOPE].reshape(kv_rank, h * MLA_NOPE)
    wvt = ukv[..., MLA_NOPE:].reshape(kv_rank, h * MLA_V).T
    return [w.astype(BF16) for w in (w1, w2, w2r, wk, wvt)]


def _rope_tables(c_len, seq):
    half = MLA_ROPE // 2
    inv_freq = 1.0 / (ROPE_THETA ** (jnp.arange(0, half, 2, dtype=F32) / half))
    pos = jnp.arange(seq, dtype=I32)
    ang_r = (pos // GRID_W).astype(F32)[:, None] * inv_freq
    ang_c = (pos % GRID_W).astype(F32)[:, None] * inv_freq
    ang = jnp.concatenate([ang_r, ang_r, ang_c, ang_c], axis=-1)
    ang = jnp.concatenate([jnp.zeros((c_len, MLA_ROPE), F32), ang], axis=0)
    return _pad_lanes(jnp.cos(ang)), _pad_lanes(jnp.sin(ang))


def _mla_proj(h, weights, g_q, g_kv, cosp, sinp, tl):
    b, lt, d = h.shape
    w1, w2, w2r, wk, wvt = weights
    q_rank, kv_rank = g_q.shape[0], g_kv.shape[0]
    nh = MLA_HEADS
    full = lambda a: pl.BlockSpec(a.shape, lambda bi, l: (0,) * a.ndim)
    scale = math.log2(math.e) / math.sqrt(MLA_NOPE + MLA_ROPE)
    return pl.pallas_call(
        functools.partial(_mla_proj_kernel, q_rank=q_rank, kv_rank=kv_rank, n_heads=nh, scale=scale),
        grid=(b, lt // tl),
        in_specs=[pl.BlockSpec((1, tl, d), lambda bi, l: (bi, l, 0)), full(w1),
                  pl.BlockSpec((1, q_rank), lambda bi, l: (0, 0)),
                  pl.BlockSpec((1, kv_rank), lambda bi, l: (0, 0)),
                  full(w2), full(w2r), full(wk), full(wvt),
                  pl.BlockSpec((tl, LANES), lambda bi, l: (l, 0)),
                  pl.BlockSpec((tl, LANES), lambda bi, l: (l, 0))],
        out_specs=[pl.BlockSpec((1, nh, tl, 2 * LANES), lambda bi, l: (bi, 0, l, 0)),
                   pl.BlockSpec((1, nh, tl, 2 * LANES), lambda bi, l: (bi, 0, l, 0)),
                   pl.BlockSpec((1, nh, MLA_V, tl), lambda bi, l: (bi, 0, 0, l))],
        out_shape=[jax.ShapeDtypeStruct((b, nh, lt, 2 * LANES), BF16),
                   jax.ShapeDtypeStruct((b, nh, lt, 2 * LANES), BF16),
                   jax.ShapeDtypeStruct((b, nh, MLA_V, lt), BF16)],
        compiler_params=_cparams(2),
        name="mla_proj",
    )(h, w1, g_q.reshape(1, q_rank), g_kv.reshape(1, kv_rank), w2, w2r, wk, wvt, cosp, sinp)


def _attn_kernel(q_ref, k_ref, vt_ref, o_ref, s0_ref, s1_ref, m0_ref, m1_ref, *, n_ctx_tiles, c_len, n_heads, ck):
    def run(lk):
        def scores(h, s_ref, m_ref):
            q = q_ref[0, h]
            m = None
            for c0 in range(0, lk, ck):
                s = lax.dot_general(k_ref[0, h, c0:c0 + ck, :], q, NT_DIMS, preferred_element_type=F32)
                s_ref[c0:c0 + ck, :] = s
                cm = jnp.max(s, axis=0, keepdims=True)
                m = cm if m is None else jnp.maximum(m, cm)
            m_ref[...] = m

        def apply_v(h, s_ref, m_ref):
            p = jnp.exp2(s_ref[0:lk, :] - m_ref[...])
            den = jnp.sum(p, axis=0, keepdims=True)
            acc = jnp.dot(vt_ref[0, h, :, 0:lk], p.astype(BF16), preferred_element_type=F32)
            o_ref[0, h] = (acc / den).T.astype(o_ref.dtype)

        def head_pair(i, carry):
            h0 = 2 * i
            scores(h0 + 1, s1_ref, m1_ref)
            apply_v(h0, s0_ref, m0_ref)
            scores(jnp.minimum(h0 + 2, n_heads - 2), s0_ref, m0_ref)
            apply_v(h0 + 1, s1_ref, m1_ref)
            return carry

        scores(0, s0_ref, m0_ref)
        lax.fori_loop(0, n_heads // 2, head_pair, 0)

    qi = pl.program_id(1)

    @pl.when(qi < n_ctx_tiles)
    def _():
        run(c_len)

    @pl.when(qi >= n_ctx_tiles)
    def _():
        run(k_ref.shape[2])


def _attention(q, k, vt, tl, c_len):
    b, nh, lt, dk = q.shape
    ck = min(ATTN_KEY_CHUNK, c_len)
    return pl.pallas_call(
        functools.partial(_attn_kernel, n_ctx_tiles=c_len // tl, c_len=c_len, n_heads=nh, ck=ck),
        grid=(b, lt // tl),
        in_specs=[pl.BlockSpec((1, nh, tl, dk), lambda bi, l: (bi, 0, l, 0)),
                  pl.BlockSpec((1, nh, lt, dk), lambda bi, l: (bi, 0, 0, 0)),
                  pl.BlockSpec((1, nh, MLA_V, lt), lambda bi, l: (bi, 0, 0, 0))],
        out_specs=pl.BlockSpec((1, nh, tl, MLA_V), lambda bi, l: (bi, 0, l, 0)),
        out_shape=jax.ShapeDtypeStruct((b, nh, lt, MLA_V), BF16),
        scratch_shapes=[pltpu.VMEM((lt, tl), F32), pltpu.VMEM((lt, tl), F32),
                        pltpu.VMEM((1, tl), F32), pltpu.VMEM((1, tl), F32)],
        compiler_params=_cparams(2),
        name="mla_attn",
    )(q, k, vt)


def _oproj_kernel(x_ref, o_ref, w_ref, g1_ref, g_ref, sh_ref, sc_ref, wrh_ref, wrl_ref, br_ref,
                  xo_ref, h2_ref, route_ref, cnt_ref, run_ref, *, n_heads, n_groups, n_exp):
    _init_run(run_ref)
    o = jnp.concatenate([o_ref[0, h] for h in range(n_heads)], axis=-1)
    mix = jnp.dot(o, w_ref[...], preferred_element_type=F32)
    xn = x_ref[0] + g1_ref[0] * mix
    _route_tail(xn, g_ref, sh_ref, sc_ref, wrh_ref, wrl_ref, br_ref, run_ref, xo_ref, h2_ref, route_ref, cnt_ref,
                n_groups=n_groups, n_exp=n_exp)


def _oproj_tail(xc, o, w_o, modt, g_ffn, router, tl, n_ctx_tiles, n_groups, n_exp):
    b, lt, d = xc.shape
    nh = o.shape[1]
    t_in, t_out, t_shape = _tail_specs(b, lt, d, tl, n_ctx_tiles, n_groups)
    return pl.pallas_call(
        functools.partial(_oproj_kernel, n_heads=nh, n_groups=n_groups, n_exp=n_exp),
        grid=(b, lt // tl),
        in_specs=[pl.BlockSpec((1, tl, d), lambda bi, l: (bi, l, 0)),
                  pl.BlockSpec((1, nh, tl, MLA_V), lambda bi, l: (bi, 0, l, 0)),
                  pl.BlockSpec(w_o.shape, lambda bi, l: (0, 0)),
                  _mod_spec(2, d, n_ctx_tiles)] + t_in,
        out_specs=t_out,
        out_shape=t_shape,
        scratch_shapes=[pltpu.VMEM((LANES, 1), F32)],
        input_output_aliases={0: 0},
        compiler_params=_cparams(2),
        name="mla_out_router",
    )(xc, o, w_o.astype(BF16), modt, g_ffn.reshape(1, d), modt, modt, *router)


def _expert_kernel(ea_ref, eb_ref, nu_ref, src_ref, src_next_ref, dst_ref, h_ref,
                   wgu_a_ref, wd_a_ref, wgu_b_ref, wd_b_ref, y_ref, xbuf, ybuf, gsem, ssem, *, tm, d, n_tok):
    del ea_ref, eb_ref
    j = pl.program_id(0)
    n_used = nu_ref[0]
    slot = j % 2
    other = 1 - slot
    n_packed = d // 2 // LANES
    hid = wd_a_ref.shape[1]

    def tile_rows(idx):
        return pl.multiple_of(idx * SUBLANES, SUBLANES)

    def gather_row(idx, s, r):
        return pltpu.make_async_copy(h_ref.at[pl.ds(tile_rows(idx), n_packed + 1)],
                                     xbuf.at[s, pl.ds(r * SUBLANES, n_packed + 1)], gsem.at[s])

    def scatter_row(idx, s, r):
        return pltpu.make_async_copy(ybuf.at[s, pl.ds(r * SUBLANES, SUBLANES)],
                                     y_ref.at[pl.ds(tile_rows(idx), SUBLANES)], ssem.at[s])

    def wait_gather(s):
        for r in range(tm):
            gather_row(0, s, r).wait()

    def wait_scatter(s):
        for r in range(tm):
            scatter_row(0, s, r).wait()

    @pl.when(j == 0)
    def _():
        ybuf[...] = jnp.zeros_like(ybuf)
        for r in range(tm):
            scatter_row(n_tok + r, 0, r).start(priority=r % 2)
            gather_row(src_ref[0, 0, r], 0, r).start(priority=r % 2)

    @pl.when(j <= n_used)
    def _():
        wait_gather(slot)
        wait_scatter(slot)
        token_row = lambda buf, k: buf[slot, pl.ds(k, tm, stride=SUBLANES), :]
        words = [token_row(xbuf, k) for k in range(n_packed)]
        unpack = lambda w: lax.bitcast_convert_type(w, F32)
        x = jnp.concatenate([unpack(w << 16) for w in words] + [unpack(w & HIGH_HALF) for w in words],
                            axis=-1).astype(BF16)
        wts = unpack(token_row(xbuf, n_packed))
        for r in range(tm):
            scatter_row(dst_ref[0, 0, r], other, r).start(priority=r % 2)
            gather_row(src_next_ref[0, 0, r], other, r).start(priority=r % 2)

        def ffn(wgu_ref, wd_ref):
            gu = jnp.dot(x, wgu_ref[0], preferred_element_type=F32)
            gate, up = gu[:, :hid], gu[:, hid:]
            act = (gate * _sigmoid(gate) * up).astype(BF16)
            return jnp.dot(act, wd_ref[0], preferred_element_type=F32)

        y = wts[:, 0:1] * ffn(wgu_a_ref, wd_a_ref) + wts[:, 1:2] * ffn(wgu_b_ref, wd_b_ref)
        for k in range(d // LANES):
            ybuf[slot, pl.ds(k, tm, stride=SUBLANES), :] = y[:, k * LANES:(k + 1) * LANES]

        @pl.when(j == n_used)
        def _():
            wait_scatter(other)
            wait_gather(other)


def _pair_tables(n_groups, n_exp):
    a, b = np.triu_indices(n_exp, k=1)
    base = (np.arange(n_groups) * n_exp)[:, None]
    return (base + a[None, :]).reshape(-1).astype(np.int32), (base + b[None, :]).reshape(-1).astype(np.int32)


def _moe(h2, route, counts, wgu, wd, layer, tm, n_groups, n_exp):
    t = h2.shape[0] // SUBLANES
    d = wgu.shape[1]
    cls_a, cls_b = _pair_tables(n_groups, n_exp)
    n_cls = cls_a.shape[0]
    n_tiles = t // tm + n_cls
    cls = route[:, 0, :].reshape(t).astype(I32)
    rank = route[:, 1, :].reshape(t).astype(I32)
    cnt = counts[:n_cls, 0].astype(I32)
    padded = ((cnt + tm - 1) // tm) * tm
    seg_end = jnp.cumsum(padded)
    seg_start = seg_end - padded
    is_cls = cls[:, None] == jnp.arange(n_cls, dtype=I32)[None, :]
    pos = jnp.sum(jnp.where(is_cls, seg_start[None, :], 0), axis=1) + rank
    rows = jnp.arange(n_tiles * tm, dtype=I32)
    spare = t + ((rows // tm) % 2) * tm + rows % tm
    dst = spare.at[pos].set(jnp.arange(t, dtype=I32))
    src = jnp.where(dst < t, dst, 0)
    n_used = (seg_end[-1] // tm).astype(I32)
    tile = jnp.arange(n_tiles + 1, dtype=I32)
    tile_cls = jnp.sum((seg_end[None, :] <= (tile * tm)[:, None]).astype(I32), axis=1)
    tile_cls = jnp.minimum(jnp.where(tile < n_used, tile_cls, tile_cls[n_used - 1]), n_cls - 1)
    first = layer * n_groups * n_exp
    ea, eb = first + jnp.asarray(cls_a)[tile_cls], first + jnp.asarray(cls_b)[tile_cls]
    dst_prev = jnp.concatenate([t + tm + jnp.arange(tm, dtype=I32), dst]).reshape(n_tiles + 1, 1, tm)
    src = src.reshape(n_tiles, 1, tm)

    two_h = wgu.shape[-1]
    idx_blk = (1, 1, tm)
    smem = lambda f: pl.BlockSpec(idx_blk, f, memory_space=pltpu.SMEM)
    wspec = lambda shape, which: pl.BlockSpec(shape, lambda j, a, bb, nu: ((a, bb)[which][j], 0, 0))
    return pl.pallas_call(
        functools.partial(_expert_kernel, tm=tm, d=d, n_tok=t),
        grid_spec=pltpu.PrefetchScalarGridSpec(
            num_scalar_prefetch=3,
            grid=(n_tiles + 1,),
            in_specs=[smem(lambda j, a, bb, nu: (0, 0, 0)),
                      smem(lambda j, a, bb, nu: (jnp.minimum(j + 1, n_tiles - 1), 0, 0)),
                      smem(lambda j, a, bb, nu: (j, 0, 0)),
                      pl.BlockSpec(memory_space=pl.ANY),
                      wspec((1, d, two_h), 0), wspec((1, two_h // 2, d), 0),
                      wspec((1, d, two_h), 1), wspec((1, two_h // 2, d), 1)],
            out_specs=pl.BlockSpec(memory_space=pl.ANY),
            scratch_shapes=[pltpu.VMEM((2, SUBLANES * tm, LANES), U32), pltpu.VMEM((2, SUBLANES * tm, LANES), F32),
                            pltpu.SemaphoreType.DMA((2,)), pltpu.SemaphoreType.DMA((2,))]),
        out_shape=jax.ShapeDtypeStruct((SUBLANES * (t + 2 * tm), LANES), F32),
        compiler_params=_cparams(1),
        name="moe_experts",
    )(ea, eb, n_used.reshape(1), src, src, dst_prev, h2, wgu, wd, wgu, wd)


def kernel(x, c, ctx, c_ctx, ada_w, ada_b, norm_mix_g, norm_ffn_g, s5_a_re, s5_a_im, s5_log_dt, s5_b_re, s5_b_im, s5_c_re, s5_c_im, s5_d, s5_w_glu, mla_w_in, mla_g_q, mla_g_kv, mla_w_uq, mla_w_ukv, mla_w_o, moe_w_group, moe_b_group, moe_w_expert, moe_b_expert, moe_w_gate_up, moe_w_down, final_g):
    b, seq, d = x.shape
    c_len = ctx.shape[1]
    depth = ada_w.shape[0]
    n_groups, n_exp = moe_w_group.shape[-1], moe_w_expert.shape[-1]
    assert n_exp == SUBLANES and n_groups < SUBLANES and n_groups * n_exp * (n_exp - 1) // 2 <= LANES
    assert seq % GRID_W == 0 and d == SUBLANES * LANES
    tl = _token_tile(c_len, seq)
    n_ctx_tiles = c_len // tl
    tm = min(MOE_TILE, tl)
    tc = min(64, tl)
    gpb = MXU_DIM // s5_b_re.shape[-1]

    xc = jnp.concatenate([ctx, x], axis=1)
    bp = -(-(b + 1) // SUBLANES) * SUBLANES
    cc = jnp.concatenate([c, c_ctx[None, :], jnp.zeros((bp - b - 1, d), F32)], axis=0)
    mod = _ada(cc, ada_w, ada_b)
    cosp, sinp = _rope_tables(c_len, seq)
    wgu_all = moe_w_gate_up.astype(BF16).reshape((-1,) + moe_w_gate_up.shape[2:])
    wd_all = moe_w_down.astype(BF16).reshape((-1,) + moe_w_down.shape[2:])

    moe = None
    for i in range(depth):
        j = i // N_MIXERS
        mod_c = jnp.broadcast_to(mod[i, b][None, :], (b, ADA_CHUNKS * d))
        modt = jnp.stack([mod_c, mod[i, :b]], axis=1).reshape(2 * b, 1, ADA_CHUNKS * d)
        router = _router_weights(moe_w_group[i], moe_b_group[i], moe_w_expert[i], moe_b_expert[i])
        if i % N_MIXERS == 0:
            xc, ht = _norm_mod(xc, norm_mix_g[i], modt, tl, n_ctx_tiles, True, moe)
            abr, abi, bbr, bbi = _s5_discretize(s5_a_re[j], s5_a_im[j], s5_log_dt[j], s5_b_re[j], s5_b_im[j])
            bm, cm, a = _s5_block_mats(abr, abi, bbr, bbi, s5_c_re[j], s5_c_im[j], gpb)
            yf, yr = _s5_scan(ht.reshape(c_len + seq, b, d), bm, cm, a, c_len, tc)
            flat = lambda y: y.reshape(c_len + seq, b * d)
            xc, h2, route, counts = _glu_tail(xc, flat(yf), flat(yr), ht, s5_d[j], s5_w_glu[j], modt,
                                              norm_ffn_g[i], router, tl, n_ctx_tiles, n_groups, n_exp)
        else:
            xc, h = _norm_mod(xc, norm_mix_g[i], modt, tl, n_ctx_tiles, False, moe)
            weights = _mla_weights(mla_w_in[j], mla_w_uq[j], mla_w_ukv[j], mla_g_q.shape[-1], mla_g_kv.shape[-1])
            q, k, vt = _mla_proj(h, weights, mla_g_q[j], mla_g_kv[j], cosp, sinp, tl)
            o = _attention(q, k, vt, tl, c_len)
            xc, h2, route, counts = _oproj_tail(xc, o, mla_w_o[j], modt, norm_ffn_g[i], router,
                                                tl, n_ctx_tiles, n_groups, n_exp)
        moe = (_moe(h2, route, counts, wgu_all, wd_all, i, tm, n_groups, n_exp), modt)
    return _final_norm(xc, moe[0], moe[1], final_g, tl, n_ctx_tiles, seq)
```

```python
import functools
import math

import jax
import jax.numpy as jnp
import numpy as np
from jax import lax
from jax.experimental import pallas as pl
from jax.experimental.pallas import tpu as pltpu

F32 = jnp.float32
BF16 = jnp.bfloat16
I32 = jnp.int32
U32 = jnp.uint32

EPS = 1e-6
GRID_W = 64
ROPE_THETA = 10000.0
ADA_CHUNKS = 6
N_MIXERS = 2
MLA_HEADS = 8
MLA_NOPE = 128
MLA_ROPE = 64
MLA_V = 128
MOE_TOP_K = 2

LANES = 128
SUBLANES = 8
MXU_DIM = 256
VMEM_LIMIT = 56 * 1024 * 1024
NEG_BIG = -1e30
NT_DIMS = (((1,), (1,)), ((), ()))
HIGH_HALF = np.uint32(0xFFFF0000)
MOE_TILE = 128
MOE_RING = 3
ATTN_KEY_CHUNK = 256


def _cparams(n_axes, vmem=VMEM_LIMIT):
    return pltpu.CompilerParams(dimension_semantics=("arbitrary",) * n_axes, vmem_limit_bytes=vmem)


def _sigmoid(x):
    return 1.0 / (1.0 + jnp.exp(-x))


def _rms(x, g):
    return x * lax.rsqrt(jnp.mean(x * x, axis=-1, keepdims=True) + EPS) * g


def _token_tile(c_len, seq):
    t = 256
    while c_len % t or seq % t:
        t //= 2
    return t


def _pad_lanes(w):
    return jnp.concatenate([w, jnp.zeros(w.shape[:-1] + (LANES - w.shape[-1],), w.dtype)], axis=-1)


def _ada_kernel(cc_ref, w_ref, b_ref, o_ref):
    cc = cc_ref[...]
    s = cc * _sigmoid(cc)
    o_ref[0] = jnp.dot(s, w_ref[0], preferred_element_type=F32, precision=lax.Precision.HIGHEST) + b_ref[0]


def _ada(cc, ada_w, ada_b):
    depth, d, n = ada_w.shape
    bp = cc.shape[0]
    tn = n // 4
    return pl.pallas_call(
        _ada_kernel,
        grid=(depth, n // tn),
        in_specs=[pl.BlockSpec((bp, d), lambda i, j: (0, 0)),
                  pl.BlockSpec((1, d, tn), lambda i, j: (i, 0, j)),
                  pl.BlockSpec((1, 1, tn), lambda i, j: (i, 0, j))],
        out_specs=pl.BlockSpec((1, bp, tn), lambda i, j: (i, 0, j)),
        out_shape=jax.ShapeDtypeStruct((depth, bp, n), F32),
        compiler_params=_cparams(2),
        name="ada_mod",
    )(cc, ada_w, ada_b.reshape(depth, 1, n))


def _mod_spec(chunk, d, n_ctx_tiles, shift=0):
    return pl.BlockSpec((1, 1, d), lambda b, l: (2 * b + jnp.where(l + shift >= n_ctx_tiles, 1, 0), 0, chunk))


def _norm_mod_kernel(x_ref, g_ref, sh_ref, sc_ref, o_ref):
    y = _rms(x_ref[0], g_ref[...])
    o_ref[...] = (y * (1.0 + sc_ref[0]) + sh_ref[0]).astype(o_ref.dtype).reshape(o_ref.shape)


def _token_rows(y_ref, tl):
    return jnp.concatenate([y_ref[pl.ds(k, tl, stride=SUBLANES), :] for k in range(SUBLANES)], axis=-1)


def _res_norm_mod_kernel(x_ref, y_ref, g2_ref, g_ref, sh_ref, sc_ref, xo_ref, o_ref):
    xn = x_ref[0] + g2_ref[0] * _token_rows(y_ref, x_ref.shape[1])
    xo_ref[0] = xn
    o_ref[...] = (_rms(xn, g_ref[...]) * (1.0 + sc_ref[0]) + sh_ref[0]).astype(o_ref.dtype).reshape(o_ref.shape)


def _norm_mod(xc, g, modt, tl, n_ctx_tiles, time_major, moe=None):
    b, lt, d = xc.shape
    nt = lt // tl
    tok = pl.BlockSpec((1, tl, d), lambda bi, l: (bi, l, 0))
    if time_major:
        h_shape = jax.ShapeDtypeStruct((lt, b * d), BF16)
        h_spec = pl.BlockSpec((tl, d), lambda bi, l: (l, bi))
    else:
        h_shape, h_spec = jax.ShapeDtypeStruct((b, lt, d), BF16), tok
    tail_specs = [pl.BlockSpec((1, d), lambda bi, l: (0, 0)),
                  _mod_spec(0, d, n_ctx_tiles), _mod_spec(1, d, n_ctx_tiles)]
    if moe is None:
        h = pl.pallas_call(
            _norm_mod_kernel, grid=(b, nt), in_specs=[tok] + tail_specs, out_specs=h_spec, out_shape=h_shape,
            compiler_params=_cparams(2), name="norm_mod",
        )(xc, g.reshape(1, d), modt, modt)
        return xc, h
    y, prev_modt = moe
    return pl.pallas_call(
        _res_norm_mod_kernel, grid=(b, nt),
        in_specs=[tok, pl.BlockSpec((SUBLANES * tl, LANES), lambda bi, l: (bi * nt + l, 0)), _mod_spec(5, d, n_ctx_tiles)] + tail_specs,
        out_specs=[tok, h_spec],
        out_shape=[jax.ShapeDtypeStruct((b, lt, d), F32), h_shape],
        input_output_aliases={0: 0},
        compiler_params=_cparams(2), name="moe_res_norm_mod",
    )(xc, y, prev_modt, g.reshape(1, d), modt, modt)


def _final_norm_kernel(x_ref, y_ref, g2_ref, g_ref, o_ref):
    o_ref[0] = _rms(x_ref[0] + g2_ref[0] * _token_rows(y_ref, x_ref.shape[1]), g_ref[...])


def _final_norm(xc, y, modt, g, tl, n_ctx_tiles, seq):
    b, lt, d = xc.shape
    nt = lt // tl
    return pl.pallas_call(
        _final_norm_kernel,
        grid=(b, seq // tl),
        in_specs=[pl.BlockSpec((1, tl, d), lambda bi, l: (bi, l + n_ctx_tiles, 0)),
                  pl.BlockSpec((SUBLANES * tl, LANES), lambda bi, l: (bi * nt + l + n_ctx_tiles, 0)),
                  _mod_spec(5, d, n_ctx_tiles, shift=n_ctx_tiles),
                  pl.BlockSpec((1, d), lambda bi, l: (0, 0))],
        out_specs=pl.BlockSpec((1, tl, d), lambda bi, l: (bi, l, 0)),
        out_shape=jax.ShapeDtypeStruct((b, seq, d), F32),
        compiler_params=_cparams(2),
        name="final_norm",
    )(xc, y, modt, g.reshape(1, d))


def _s5_disc_kernel(ar_ref, ai_ref, ldt_ref, br_ref, bi_ref, abr_ref, abi_ref, bbr_ref, bbi_ref):
    ar, ai = ar_ref[...], ai_ref[...]
    dt = jnp.exp(ldt_ref[...])
    ldr, ldi = ar * dt, ai * dt
    mag = jnp.exp(ldr)
    abr, abi = mag * jnp.cos(ldi), mag * jnp.sin(ldi)
    nr, ni = abr - 1.0, abi
    den = ar * ar + ai * ai
    qr = (nr * ar + ni * ai) / den
    qi = (ni * ar - nr * ai) / den
    br, bi = br_ref[...], bi_ref[...]
    abr_ref[...] = abr
    abi_ref[...] = abi
    bbr_ref[...] = qr * br - qi * bi
    bbi_ref[...] = qr * bi + qi * br


def _s5_discretize(a_re, a_im, log_dt, b_re, b_im):
    two, g, p, c = b_re.shape
    rows, cols = two * g, p * c
    wide = lambda a: jnp.broadcast_to(a[..., None], (two, g, p, c)).reshape(rows, cols)
    ldt = jnp.broadcast_to(log_dt[:, :, None, None], (two, g, p, c)).reshape(rows, cols)
    spec = pl.BlockSpec((rows, cols), lambda: (0, 0))
    shape = jax.ShapeDtypeStruct((rows, cols), F32)
    abr, abi, bbr, bbi = pl.pallas_call(
        _s5_disc_kernel,
        in_specs=[spec] * 5,
        out_specs=[spec] * 4,
        out_shape=[shape] * 4,
        name="s5_disc",
    )(wide(a_re), wide(a_im), ldt, b_re.reshape(rows, cols), b_im.reshape(rows, cols))
    pick = lambda a: a.reshape(two, g, p, c)[..., 0]
    return pick(abr), pick(abi), bbr.reshape(two, g, p, c), bbi.reshape(two, g, p, c)


def _s5_block_mats(abr, abi, bbr, bbi, c_re, c_im, gpb):
    two, g, p, c = bbr.shape
    ncb = g // gpb
    eye = jnp.eye(gpb, dtype=F32)

    def in_mat(bb):
        bb = bb.reshape(two, ncb, gpb, p, c)
        return jnp.einsum('dngpc,gh->dngchp', bb, eye).reshape(two, ncb, gpb * c, gpb * p)

    def out_mat(cc):
        cc = cc.reshape(two, ncb, gpb, c, p)
        return jnp.einsum('dngcp,gh->dngphc', cc, eye).reshape(two, ncb, gpb * p, gpb * c)

    bm = jnp.concatenate([in_mat(bbr), in_mat(bbi)], axis=-1).astype(BF16)
    cm = jnp.concatenate([out_mat(c_re), -out_mat(c_im)], axis=-2).astype(BF16)
    a = jnp.stack([abr.reshape(two, ncb, gpb * p), abi.reshape(two, ncb, gpb * p)], axis=2)
    return bm, cm, a


def _s5_kernel(xf_ref, xr_ref, bm_ref, cm_ref, a_ref, yf_ref, yr_ref, buf_ref, bur_ref, st_ref,
               *, tc, nb, cw, ns, rb, lw):
    @pl.when(pl.program_id(1) == 0)
    def _():
        st_ref[...] = jnp.zeros_like(st_ref)

    tpb = rb // nb
    nrb = tc // tpb
    refs = ((xf_ref, yf_ref, buf_ref), (xr_ref, yr_ref, bur_ref))
    order = (tuple(range(nrb)), tuple(range(nrb - 1, -1, -1)))

    def inject(d, r):
        x_ref, _, bu_ref = refs[d]
        x = x_ref[r * tpb:(r + 1) * tpb].reshape(rb, cw)
        bu_ref[r * rb:(r + 1) * rb, :] = jnp.dot(x, bm_ref[d, 0], preferred_element_type=F32)

    def recur(d, r):
        bu_ref = refs[d][2]
        for lc in range(ns // lw):
            lo = lc * lw
            are = jnp.broadcast_to(a_ref[d, 0, 0:1, lo:lo + lw], (nb, lw))
            aim = jnp.broadcast_to(a_ref[d, 0, 1:2, lo:lo + lw], (nb, lw))
            hr, hi = st_ref[d, 0, :, lo:lo + lw], st_ref[d, 1, :, lo:lo + lw]
            for i in range(tpb):
                t = r * tpb + (i if d == 0 else tpb - 1 - i)
                rows = slice(t * nb, (t + 1) * nb)
                hr, hi = (are * hr - aim * hi + bu_ref[rows, lo:lo + lw],
                          are * hi + aim * hr + bu_ref[rows, ns + lo:ns + lo + lw])
                bu_ref[rows, lo:lo + lw] = hr
                bu_ref[rows, ns + lo:ns + lo + lw] = hi
            st_ref[d, 0, :, lo:lo + lw] = hr
            st_ref[d, 1, :, lo:lo + lw] = hi

    def readout(d, r):
        _, y_ref, bu_ref = refs[d]
        h = bu_ref[r * rb:(r + 1) * rb, :].astype(BF16)
        y = jnp.dot(h, cm_ref[d, 0], preferred_element_type=F32)
        y_ref[r * tpb:(r + 1) * tpb] = y.reshape(tpb, nb, cw).astype(y_ref.dtype)

    for d in range(2):
        inject(d, order[d][0])
    for k in range(nrb):
        for d in range(2):
            if k + 1 < nrb:
                inject(d, order[d][k + 1])
        for d in range(2):
            recur(d, order[d][k])
        for d in range(2):
            if k >= 1:
                readout(d, order[d][k - 1])
    for d in range(2):
        readout(d, order[d][nrb - 1])


def _s5_scan(ht3, bm, cm, a, c_len, tc):
    lt, nb, d = ht3.shape
    _, ncb, cw, ns2 = bm.shape
    ns = ns2 // 2
    n_c, n_all = c_len // tc, lt // tc
    rb = min(256, tc * nb)

    def rev(k):
        return jnp.where(k < n_c, n_c - 1 - k, n_all - 1 - (k - n_c))

    x_blk = (tc, nb, cw)
    kern = functools.partial(_s5_kernel, tc=tc, nb=nb, cw=cw, ns=ns, rb=rb, lw=min(256, ns))
    return pl.pallas_call(
        kern,
        grid=(ncb, n_all),
        in_specs=[pl.BlockSpec(x_blk, lambda j, k: (k, 0, j)),
                  pl.BlockSpec(x_blk, lambda j, k: (rev(k), 0, j)),
                  pl.BlockSpec((2, 1, cw, ns2), lambda j, k: (0, j, 0, 0)),
                  pl.BlockSpec((2, 1, ns2, cw), lambda j, k: (0, j, 0, 0)),
                  pl.BlockSpec((2, 1, 2, ns), lambda j, k: (0, j, 0, 0))],
        out_specs=[pl.BlockSpec(x_blk, lambda j, k: (k, 0, j)),
                   pl.BlockSpec(x_blk, lambda j, k: (rev(k), 0, j))],
        out_shape=[jax.ShapeDtypeStruct((lt, nb, d), BF16)] * 2,
        scratch_shapes=[pltpu.VMEM((tc * nb, ns2), F32), pltpu.VMEM((tc * nb, ns2), F32),
                        pltpu.VMEM((2, 2, nb, ns), F32)],
        compiler_params=_cparams(2),
        name="s5_scan",
    )(ht3, ht3, bm, cm, a)


def _route_tail(xn, g_ref, sh_ref, sc_ref, wrh_ref, wrl_ref, br_ref, run_ref, xo_ref, h2_ref, route_ref, cnt_ref,
                *, n_groups, n_exp):
    tl, d = xn.shape
    xo_ref[0] = xn
    h2 = _rms(xn, g_ref[...]) * (1.0 + sc_ref[0]) + sh_ref[0]
    h_hi = h2.astype(BF16)
    h_lo = (h2 - h_hi.astype(F32)).astype(BF16)
    dg = lambda w, x: lax.dot_general(w, x, NT_DIMS, preferred_element_type=F32)
    logits = dg(wrh_ref[...], h_hi) + (dg(wrh_ref[...], h_lo) + dg(wrl_ref[...], h_hi)) + br_ref[...]

    row = lax.broadcasted_iota(I32, (SUBLANES, tl), 0)

    def softmax0(v, mask):
        v = jnp.where(mask, v, NEG_BIG)
        e = jnp.where(mask, jnp.exp(v - jnp.max(v, axis=0, keepdims=True)), 0.0)
        return e / jnp.sum(e, axis=0, keepdims=True)

    def top1(p, mask):
        v = jnp.max(jnp.where(mask, p, -1.0), axis=0, keepdims=True)
        idx = jnp.min(jnp.where(mask & (p == v), row, SUBLANES), axis=0, keepdims=True)
        return v, idx

    is_g = row < n_groups
    p_g, g_idx = top1(softmax0(logits[0:SUBLANES], is_g), is_g)
    le = logits[SUBLANES:2 * SUBLANES]
    for g in range(1, n_groups):
        le = jnp.where(g_idx == g, logits[(g + 1) * SUBLANES:(g + 2) * SUBLANES], le)
    every = row >= 0
    pe = softmax0(le, every)
    p0, i0 = top1(pe, every)
    p1, i1 = top1(pe, row != i0)
    den = p0 + p1
    w0, w1 = p_g * (p0 / den), p_g * (p1 / den)
    first_low = i0 < i1
    ea, eb = jnp.minimum(i0, i1), jnp.maximum(i0, i1)
    wa, wb = jnp.where(first_low, w0, w1), jnp.where(first_low, w1, w0)
    n_pairs = n_exp * (n_exp - 1) // 2
    pair = ea * (n_exp - 1) - jnp.right_shift(ea * (ea - 1), 1) + (eb - ea - 1)
    cls = g_idx * n_pairs + pair

    crow = lax.broadcasted_iota(I32, (LANES, tl), 0)
    hit = crow == cls
    onehot = jnp.where(hit, 1.0, 0.0)
    src = lax.broadcasted_iota(I32, (tl, tl), 0)
    tgt = lax.broadcasted_iota(I32, (tl, tl), 1)
    earlier = jnp.where(src < tgt, 1.0, 0.0).astype(BF16)
    before = jnp.dot(onehot.astype(BF16), earlier, preferred_element_type=F32) + run_ref[...]
    rank = jnp.sum(jnp.where(hit, before, 0.0), axis=0, keepdims=True)
    run_ref[...] = run_ref[...] + jnp.sum(onehot, axis=1, keepdims=True)
    cnt_ref[...] = jnp.broadcast_to(run_ref[...], cnt_ref.shape)

    route = jnp.zeros((SUBLANES, tl), F32)
    for i, v in enumerate((cls.astype(F32), rank, wa, wb)):
        route = jnp.where(row == i, v, route)
    route_ref[0] = route

    diag = src == tgt
    wa_col = jnp.sum(jnp.where(diag, wa, 0.0), axis=1, keepdims=True)
    wb_col = jnp.sum(jnp.where(diag, wb, 0.0), axis=1, keepdims=True)
    lane = lax.broadcasted_iota(I32, (tl, LANES), 1)
    wrow = jnp.where(lane == 0, wa_col, jnp.where(lane == 1, wb_col, 0.0))
    bits = lax.bitcast_convert_type(h_hi.astype(F32), U32)
    half = d // 2
    packed = (bits[:, 0:half] >> 16) | (bits[:, half:d] & HIGH_HALF)
    n_packed = half // LANES
    for c in range(n_packed):
        h2_ref[pl.ds(c, tl, stride=SUBLANES), :] = packed[:, c * LANES:(c + 1) * LANES]
    h2_ref[pl.ds(n_packed, tl, stride=SUBLANES), :] = lax.bitcast_convert_type(wrow, U32)
    for c in range(n_packed + 1, SUBLANES):
        h2_ref[pl.ds(c, tl, stride=SUBLANES), :] = jnp.zeros((tl, LANES), U32)


def _route_rows(n_groups):
    return -(-(SUBLANES * (n_groups + 1)) // 16) * 16


def _tail_specs(b, lt, d, tl, n_ctx_tiles, n_groups):
    nt = lt // tl
    rr = _route_rows(n_groups)
    const = lambda shape: pl.BlockSpec(shape, lambda bi, l: (0,) * len(shape))
    in_specs = [const((1, d)),
                _mod_spec(3, d, n_ctx_tiles),
                _mod_spec(4, d, n_ctx_tiles),
                const((rr, d)), const((rr, d)), const((rr, 1))]
    out_specs = [pl.BlockSpec((1, tl, d), lambda bi, l: (bi, l, 0)),
                 pl.BlockSpec((SUBLANES * tl, LANES), lambda bi, l: (bi * nt + l, 0)),
                 pl.BlockSpec((1, SUBLANES, tl), lambda bi, l: (bi * nt + l, 0, 0)),
                 const((LANES, LANES))]
    out_shape = [jax.ShapeDtypeStruct((b, lt, d), F32),
                 jax.ShapeDtypeStruct((SUBLANES * b * lt, LANES), U32),
                 jax.ShapeDtypeStruct((b * nt, SUBLANES, tl), F32),
                 jax.ShapeDtypeStruct((LANES, LANES), F32)]
    return in_specs, out_specs, out_shape


def _init_run(run_ref):
    @pl.when((pl.program_id(0) == 0) & (pl.program_id(1) == 0))
    def _():
        run_ref[...] = jnp.zeros_like(run_ref)


def _router_weights(w_group, b_group, w_expert, b_expert):
    d, n_groups = w_group.shape
    n_exp = w_expert.shape[-1]
    rr = _route_rows(n_groups)
    w = jnp.zeros((rr, d), F32).at[0:n_groups].set(w_group.T)
    w = w.at[SUBLANES:SUBLANES * (n_groups + 1)].set(jnp.transpose(w_expert, (0, 2, 1)).reshape(n_groups * n_exp, d))
    bias = jnp.zeros((rr, 1), F32).at[0:n_groups, 0].set(b_group)
    bias = bias.at[SUBLANES:SUBLANES * (n_groups + 1), 0].set(b_expert.reshape(-1))
    w_hi = w.astype(BF16)
    return w_hi, (w - w_hi.astype(F32)).astype(BF16), bias


def _glu_kernel(x_ref, yf_ref, yr_ref, h_ref, dd_ref, w_ref, g1_ref,
                g_ref, sh_ref, sc_ref, wrh_ref, wrl_ref, br_ref,
                xo_ref, h2_ref, route_ref, cnt_ref, run_ref, *, n_groups, n_exp):
    _init_run(run_ref)
    d = x_ref.shape[-1]
    u = yf_ref[...].astype(F32) + yr_ref[...].astype(F32) + dd_ref[...] * h_ref[...].astype(F32)
    z = 0.5 * u * (1.0 + jnp.tanh(math.sqrt(2.0 / math.pi) * (u + 0.044715 * (u * u * u))))
    o = jnp.dot(z.astype(BF16), w_ref[...], preferred_element_type=F32)
    mix = o[:, :d] * _sigmoid(o[:, d:])
    xn = x_ref[0] + g1_ref[0] * mix
    _route_tail(xn, g_ref, sh_ref, sc_ref, wrh_ref, wrl_ref, br_ref, run_ref, xo_ref, h2_ref, route_ref, cnt_ref,
                n_groups=n_groups, n_exp=n_exp)


def _glu_tail(xc, yft, yrt, ht, s5_d, w_glu, modt, g_ffn, router, tl, n_ctx_tiles, n_groups, n_exp):
    b, lt, d = xc.shape
    tm_spec = pl.BlockSpec((tl, d), lambda bi, l: (l, bi))
    t_in, t_out, t_shape = _tail_specs(b, lt, d, tl, n_ctx_tiles, n_groups)
    return pl.pallas_call(
        functools.partial(_glu_kernel, n_groups=n_groups, n_exp=n_exp),
        grid=(b, lt // tl),
        in_specs=[pl.BlockSpec((1, tl, d), lambda bi, l: (bi, l, 0)), tm_spec, tm_spec, tm_spec,
                  pl.BlockSpec((1, d), lambda bi, l: (0, 0)),
                  pl.BlockSpec((d, 2 * d), lambda bi, l: (0, 0)),
                  _mod_spec(2, d, n_ctx_tiles)] + t_in,
        out_specs=t_out,
        out_shape=t_shape,
        scratch_shapes=[pltpu.VMEM((LANES, 1), F32)],
        input_output_aliases={0: 0},
        compiler_params=_cparams(2),
        name="s5_glu_router",
    )(xc, yft, yrt, ht, s5_d.reshape(1, d), w_glu.astype(BF16), modt,
      g_ffn.reshape(1, d), modt, modt, *router)


def _mla_proj_kernel(h_ref, w1_ref, gq_ref, gkv_ref, w2_ref, w2r_ref, wk_ref, wvt_ref, cos_ref, sin_ref,
                     q_ref, k_ref, vt_ref, *, q_rank, kv_rank, n_heads, scale):
    p = jnp.dot(h_ref[0], w1_ref[...], preferred_element_type=F32)
    o_kr = q_rank + kv_rank
    cqn = _rms(p[:, :q_rank], gq_ref[...]).astype(BF16)
    ckvn = _rms(p[:, q_rank:o_kr], gkv_ref[...]).astype(BF16)
    cosp, sinp = cos_ref[...], sin_ref[...]
    kr = (p[:, o_kr:o_kr + LANES] * cosp + p[:, o_kr + LANES:o_kr + 2 * LANES] * sinp).astype(BF16)
    qm = jnp.dot(cqn, w2_ref[...], preferred_element_type=F32)
    qrot = jnp.dot(cqn, w2r_ref[...], preferred_element_type=F32)
    kn = jnp.dot(ckvn, wk_ref[...], preferred_element_type=F32)
    vt = lax.dot_general(wvt_ref[...], ckvn, NT_DIMS, preferred_element_type=F32)
    hn = n_heads * MLA_NOPE
    for h in range(n_heads):
        s = slice(h * LANES, (h + 1) * LANES)
        sr = slice(hn + h * LANES, hn + (h + 1) * LANES)
        q_ref[0, h, :, 0:LANES] = (qm[:, s] * scale).astype(BF16)
        q_ref[0, h, :, LANES:2 * LANES] = ((qm[:, sr] * cosp + qrot[:, s] * sinp) * scale).astype(BF16)
        k_ref[0, h, :, 0:LANES] = kn[:, s].astype(BF16)
        k_ref[0, h, :, LANES:2 * LANES] = kr
        vt_ref[0, h] = vt[h * MLA_V:(h + 1) * MLA_V].astype(BF16)


def _rot_cols(w):
    a1, a2, b1, b2 = jnp.split(w, 4, axis=-1)
    return jnp.concatenate([-a2, a1, -b2, b1], axis=-1)


def _mla_weights(w_in, w_uq, w_ukv, q_rank, kv_rank):
    h = MLA_HEADS
    o_kr = q_rank + kv_rank
    w_kr = w_in[:, o_kr:]
    w1 = jnp.concatenate([w_in[:, :o_kr], _pad_lanes(w_kr), _pad_lanes(_rot_cols(w_kr))], axis=-1)
    uq = w_uq.reshape(q_rank, h, MLA_NOPE + MLA_ROPE)
    uq_n, uq_r = uq[..., :MLA_NOPE], uq[..., MLA_NOPE:]
    w2 = jnp.concatenate([uq_n.reshape(q_rank, h * MLA_NOPE), _pad_lanes(uq_r).reshape(q_rank, h * LANES)], axis=-1)
    w2r = _pad_lanes(_rot_cols(uq_r)).reshape(q_rank, h * LANES)
    ukv = w_ukv.reshape(kv_rank, h, MLA_NOPE + MLA_V)
    wk = ukv[..., :MLA_NOPE].reshape(kv_rank, h * MLA_NOPE)
    wvt = ukv[..., MLA_NOPE:].reshape(kv_rank, h * MLA_V).T
    return [w.astype(BF16) for w in (w1, w2, w2r, wk, wvt)]


def _rope_tables(c_len, seq):
    half = MLA_ROPE // 2
    inv_freq = 1.0 / (ROPE_THETA ** (jnp.arange(0, half, 2, dtype=F32) / half))
    pos = jnp.arange(seq, dtype=I32)
    ang_r = (pos // GRID_W).astype(F32)[:, None] * inv_freq
    ang_c = (pos % GRID_W).astype(F32)[:, None] * inv_freq
    ang = jnp.concatenate([ang_r, ang_r, ang_c, ang_c], axis=-1)
    ang = jnp.concatenate([jnp.zeros((c_len, MLA_ROPE), F32), ang], axis=0)
    return _pad_lanes(jnp.cos(ang)), _pad_lanes(jnp.sin(ang))


def _mla_proj(h, weights, g_q, g_kv, cosp, sinp, tl):
    b, lt, d = h.shape
    w1, w2, w2r, wk, wvt = weights
    q_rank, kv_rank = g_q.shape[0], g_kv.shape[0]
    nh = MLA_HEADS
    full = lambda a: pl.BlockSpec(a.shape, lambda bi, l: (0,) * a.ndim)
    scale = math.log2(math.e) / math.sqrt(MLA_NOPE + MLA_ROPE)
    return pl.pallas_call(
        functools.partial(_mla_proj_kernel, q_rank=q_rank, kv_rank=kv_rank, n_heads=nh, scale=scale),
        grid=(b, lt // tl),
        in_specs=[pl.BlockSpec((1, tl, d), lambda bi, l: (bi, l, 0)), full(w1),
                  pl.BlockSpec((1, q_rank), lambda bi, l: (0, 0)),
                  pl.BlockSpec((1, kv_rank), lambda bi, l: (0, 0)),
                  full(w2), full(w2r), full(wk), full(wvt),
                  pl.BlockSpec((tl, LANES), lambda bi, l: (l, 0)),
                  pl.BlockSpec((tl, LANES), lambda bi, l: (l, 0))],
        out_specs=[pl.BlockSpec((1, nh, tl, 2 * LANES), lambda bi, l: (bi, 0, l, 0)),
                   pl.BlockSpec((1, nh, tl, 2 * LANES), lambda bi, l: (bi, 0, l, 0)),
                   pl.BlockSpec((1, nh, MLA_V, tl), lambda bi, l: (bi, 0, 0, l))],
        out_shape=[jax.ShapeDtypeStruct((b, nh, lt, 2 * LANES), BF16),
                   jax.ShapeDtypeStruct((b, nh, lt, 2 * LANES), BF16),
                   jax.ShapeDtypeStruct((b, nh, MLA_V, lt), BF16)],
        compiler_params=_cparams(2),
        name="mla_proj",
    )(h, w1, g_q.reshape(1, q_rank), g_kv.reshape(1, kv_rank), w2, w2r, wk, wvt, cosp, sinp)


def _attn_kernel(q_ref, k_ref, vt_ref, o_ref, s0_ref, s1_ref, m0_ref, m1_ref, *, n_ctx_tiles, c_len, n_heads, ck):
    def run(lk):
        def scores(h, s_ref, m_ref):
            q = q_ref[0, h]
            m = None
            for c0 in range(0, lk, ck):
                s = lax.dot_general(k_ref[0, h, c0:c0 + ck, :], q, NT_DIMS, preferred_element_type=F32)
                s_ref[c0:c0 + ck, :] = s
                cm = jnp.max(s, axis=0, keepdims=True)
                m = cm if m is None else jnp.maximum(m, cm)
            m_ref[...] = m

        def apply_v(h, s_ref, m_ref):
            p = jnp.exp2(s_ref[0:lk, :] - m_ref[...])
            den = jnp.sum(p, axis=0, keepdims=True)
            acc = jnp.dot(vt_ref[0, h, :, 0:lk], p.astype(BF16), preferred_element_type=F32)
            o_ref[0, h] = (acc / den).T.astype(o_ref.dtype)

        def head_pair(i, carry):
            h0 = 2 * i
            scores(h0 + 1, s1_ref, m1_ref)
            apply_v(h0, s0_ref, m0_ref)
            scores(jnp.minimum(h0 + 2, n_heads - 2), s0_ref, m0_ref)
            apply_v(h0 + 1, s1_ref, m1_ref)
            return carry

        scores(0, s0_ref, m0_ref)
        lax.fori_loop(0, n_heads // 2, head_pair, 0)

    qi = pl.program_id(1)

    @pl.when(qi < n_ctx_tiles)
    def _():
        run(c_len)

    @pl.when(qi >= n_ctx_tiles)
    def _():
        run(k_ref.shape[2])


def _attention(q, k, vt, tl, c_len):
    b, nh, lt, dk = q.shape
    ck = min(ATTN_KEY_CHUNK, c_len)
    return pl.pallas_call(
        functools.partial(_attn_kernel, n_ctx_tiles=c_len // tl, c_len=c_len, n_heads=nh, ck=ck),
        grid=(b, lt // tl),
        in_specs=[pl.BlockSpec((1, nh, tl, dk), lambda bi, l: (bi, 0, l, 0)),
                  pl.BlockSpec((1, nh, lt, dk), lambda bi, l: (bi, 0, 0, 0)),
                  pl.BlockSpec((1, nh, MLA_V, lt), lambda bi, l: (bi, 0, 0, 0))],
        out_specs=pl.BlockSpec((1, nh, tl, MLA_V), lambda bi, l: (bi, 0, l, 0)),
        out_shape=jax.ShapeDtypeStruct((b, nh, lt, MLA_V), BF16),
        scratch_shapes=[pltpu.VMEM((lt, tl), F32), pltpu.VMEM((lt, tl), F32),
                        pltpu.VMEM((1, tl), F32), pltpu.VMEM((1, tl), F32)],
        compiler_params=_cparams(2),
        name="mla_attn",
    )(q, k, vt)


def _oproj_kernel(x_ref, o_ref, w_ref, g1_ref, g_ref, sh_ref, sc_ref, wrh_ref, wrl_ref, br_ref,
                  xo_ref, h2_ref, route_ref, cnt_ref, run_ref, *, n_heads, n_groups, n_exp):
    _init_run(run_ref)
    o = jnp.concatenate([o_ref[0, h] for h in range(n_heads)], axis=-1)
    mix = jnp.dot(o, w_ref[...], preferred_element_type=F32)
    xn = x_ref[0] + g1_ref[0] * mix
    _route_tail(xn, g_ref, sh_ref, sc_ref, wrh_ref, wrl_ref, br_ref, run_ref, xo_ref, h2_ref, route_ref, cnt_ref,
                n_groups=n_groups, n_exp=n_exp)


def _oproj_tail(xc, o, w_o, modt, g_ffn, router, tl, n_ctx_tiles, n_groups, n_exp):
    b, lt, d = xc.shape
    nh = o.shape[1]
    t_in, t_out, t_shape = _tail_specs(b, lt, d, tl, n_ctx_tiles, n_groups)
    return pl.pallas_call(
        functools.partial(_oproj_kernel, n_heads=nh, n_groups=n_groups, n_exp=n_exp),
        grid=(b, lt // tl),
        in_specs=[pl.BlockSpec((1, tl, d), lambda bi, l: (bi, l, 0)),
                  pl.BlockSpec((1, nh, tl, MLA_V), lambda bi, l: (bi, 0, l, 0)),
                  pl.BlockSpec(w_o.shape, lambda bi, l: (0, 0)),
                  _mod_spec(2, d, n_ctx_tiles)] + t_in,
        out_specs=t_out,
        out_shape=t_shape,
        scratch_shapes=[pltpu.VMEM((LANES, 1), F32)],
        input_output_aliases={0: 0},
        compiler_params=_cparams(2),
        name="mla_out_router",
    )(xc, o, w_o.astype(BF16), modt, g_ffn.reshape(1, d), modt, modt, *router)


def _expert_kernel(ea_ref, eb_ref, nu_ref, src0_ref, src1_ref, src_ahead_ref, dst_ref, h_ref,
                   wgu_a_ref, wd_a_ref, wgu_b_ref, wd_b_ref, y_ref, xbuf, ybuf, gsem, ssem, *, tm, d, n_tok):
    del ea_ref, eb_ref
    j = pl.program_id(0)
    n_used = nu_ref[0]
    slot = j % MOE_RING
    ahead = (j + 2) % MOE_RING
    mid = (j + 1) % MOE_RING
    n_packed = d // 2 // LANES
    hid = wd_a_ref.shape[1]

    def tile_rows(idx):
        return pl.multiple_of(idx * SUBLANES, SUBLANES)

    def gather_row(idx, s, r):
        return pltpu.make_async_copy(h_ref.at[pl.ds(tile_rows(idx), n_packed + 1)],
                                     xbuf.at[s, pl.ds(r * SUBLANES, n_packed + 1)], gsem.at[s])

    def scatter_row(idx, s, r):
        return pltpu.make_async_copy(ybuf.at[s, pl.ds(r * SUBLANES, SUBLANES)],
                                     y_ref.at[pl.ds(tile_rows(idx), SUBLANES)], ssem.at[s])

    def wait_gather(s):
        for r in range(tm):
            gather_row(0, s, r).wait()

    def wait_scatter(s):
        for r in range(tm):
            scatter_row(0, s, r).wait()

    @pl.when(j == 0)
    def _():
        ybuf[...] = jnp.zeros_like(ybuf)
        for r in range(tm):
            scatter_row(n_tok + r, 0, r).start(priority=r % 2)
            scatter_row(n_tok + tm + r, 1, r).start(priority=r % 2)
            gather_row(src0_ref[0, 0, r], 0, r).start(priority=r % 2)
            gather_row(src1_ref[0, 0, r], 1, r).start(priority=r % 2)

    @pl.when(j <= n_used)
    def _():
        wait_gather(slot)
        wait_scatter(slot)
        token_row = lambda buf, k: buf[slot, pl.ds(k, tm, stride=SUBLANES), :]
        words = [token_row(xbuf, k) for k in range(n_packed)]
        unpack = lambda w: lax.bitcast_convert_type(w, F32)
        x = jnp.concatenate([unpack(w << 16) for w in words] + [unpack(w & HIGH_HALF) for w in words],
                            axis=-1).astype(BF16)
        wts = unpack(token_row(xbuf, n_packed))
        for r in range(tm):
            scatter_row(dst_ref[0, 0, r], ahead, r).start(priority=r % 2)
            gather_row(src_ahead_ref[0, 0, r], ahead, r).start(priority=r % 2)

        def ffn(wgu_ref, wd_ref):
            gu = jnp.dot(x, wgu_ref[0], preferred_element_type=F32)
            gate, up = gu[:, :hid], gu[:, hid:]
            act = (gate * _sigmoid(gate) * up).astype(BF16)
            return jnp.dot(act, wd_ref[0], preferred_element_type=F32)

        y = wts[:, 0:1] * ffn(wgu_a_ref, wd_a_ref) + wts[:, 1:2] * ffn(wgu_b_ref, wd_b_ref)
        for k in range(d // LANES):
            ybuf[slot, pl.ds(k, tm, stride=SUBLANES), :] = y[:, k * LANES:(k + 1) * LANES]

        @pl.when(j == n_used)
        def _():
            wait_scatter(mid)
            wait_scatter(ahead)
            wait_gather(mid)
            wait_gather(ahead)


def _pair_tables(n_groups, n_exp):
    a, b = np.triu_indices(n_exp, k=1)
    base = (np.arange(n_groups) * n_exp)[:, None]
    return (base + a[None, :]).reshape(-1).astype(np.int32), (base + b[None, :]).reshape(-1).astype(np.int32)


def _moe(h2, route, counts, wgu, wd, layer, tm, n_groups, n_exp):
    t = h2.shape[0] // SUBLANES
    d = wgu.shape[1]
    cls_a, cls_b = _pair_tables(n_groups, n_exp)
    n_cls = cls_a.shape[0]
    n_tiles = t // tm + n_cls
    cls = route[:, 0, :].reshape(t).astype(I32)
    rank = route[:, 1, :].reshape(t).astype(I32)
    cnt = counts[:n_cls, 0].astype(I32)
    padded = ((cnt + tm - 1) // tm) * tm
    seg_end = jnp.cumsum(padded)
    seg_start = seg_end - padded
    is_cls = cls[:, None] == jnp.arange(n_cls, dtype=I32)[None, :]
    pos = jnp.sum(jnp.where(is_cls, seg_start[None, :], 0), axis=1) + rank
    rows = jnp.arange(n_tiles * tm, dtype=I32)
    spare = t + ((rows // tm) % MOE_RING) * tm + rows % tm
    dst = spare.at[pos].set(jnp.arange(t, dtype=I32))
    src = jnp.where(dst < t, dst, 0)
    n_used = (seg_end[-1] // tm).astype(I32)
    tile = jnp.arange(n_tiles + 1, dtype=I32)
    tile_cls = jnp.sum((seg_end[None, :] <= (tile * tm)[:, None]).astype(I32), axis=1)
    tile_cls = jnp.minimum(jnp.where(tile < n_used, tile_cls, tile_cls[n_used - 1]), n_cls - 1)
    first = layer * n_groups * n_exp
    ea, eb = first + jnp.asarray(cls_a)[tile_cls], first + jnp.asarray(cls_b)[tile_cls]
    dst_prev = jnp.concatenate([t + (MOE_RING - 1) * tm + jnp.arange(tm, dtype=I32), dst])
    dst_prev = dst_prev.reshape(n_tiles + 1, 1, tm)
    src = src.reshape(n_tiles, 1, tm)

    two_h = wgu.shape[-1]
    idx_blk = (1, 1, tm)
    smem = lambda f: pl.BlockSpec(idx_blk, f, memory_space=pltpu.SMEM)
    wspec = lambda shape, which: pl.BlockSpec(shape, lambda j, a, bb, nu: ((a, bb)[which][j], 0, 0))
    return pl.pallas_call(
        functools.partial(_expert_kernel, tm=tm, d=d, n_tok=t),
        grid_spec=pltpu.PrefetchScalarGridSpec(
            num_scalar_prefetch=3,
            grid=(n_tiles + 1,),
            in_specs=[smem(lambda j, a, bb, nu: (0, 0, 0)),
                      smem(lambda j, a, bb, nu: (min(1, n_tiles - 1), 0, 0)),
                      smem(lambda j, a, bb, nu: (jnp.minimum(j + 2, n_tiles - 1), 0, 0)),
                      smem(lambda j, a, bb, nu: (j, 0, 0)),
                      pl.BlockSpec(memory_space=pl.ANY),
                      wspec((1, d, two_h), 0), wspec((1, two_h // 2, d), 0),
                      wspec((1, d, two_h), 1), wspec((1, two_h // 2, d), 1)],
            out_specs=pl.BlockSpec(memory_space=pl.ANY),
            scratch_shapes=[pltpu.VMEM((MOE_RING, SUBLANES * tm, LANES), U32),
                            pltpu.VMEM((MOE_RING, SUBLANES * tm, LANES), F32),
                            pltpu.SemaphoreType.DMA((MOE_RING,)), pltpu.SemaphoreType.DMA((MOE_RING,))]),
        out_shape=jax.ShapeDtypeStruct((SUBLANES * (t + MOE_RING * tm), LANES), F32),
        compiler_params=_cparams(1),
        name="moe_experts",
    )(ea, eb, n_used.reshape(1), src, src, src, dst_prev, h2, wgu, wd, wgu, wd)


def kernel(x, c, ctx, c_ctx, ada_w, ada_b, norm_mix_g, norm_ffn_g, s5_a_re, s5_a_im, s5_log_dt, s5_b_re, s5_b_im, s5_c_re, s5_c_im, s5_d, s5_w_glu, mla_w_in, mla_g_q, mla_g_kv, mla_w_uq, mla_w_ukv, mla_w_o, moe_w_group, moe_b_group, moe_w_expert, moe_b_expert, moe_w_gate_up, moe_w_down, final_g):
    b, seq, d = x.shape
    c_len = ctx.shape[1]
    depth = ada_w.shape[0]
    n_groups, n_exp = moe_w_group.shape[-1], moe_w_expert.shape[-1]
    assert n_exp == SUBLANES and n_groups < SUBLANES and n_groups * n_exp * (n_exp - 1) // 2 <= LANES
    assert seq % GRID_W == 0 and d == SUBLANES * LANES
    tl = _token_tile(c_len, seq)
    n_ctx_tiles = c_len // tl
    tm = min(MOE_TILE, tl)
    tc = min(64, tl)
    gpb = MXU_DIM // s5_b_re.shape[-1]

    xc = jnp.concatenate([ctx, x], axis=1)
    bp = -(-(b + 1) // SUBLANES) * SUBLANES
    cc = jnp.concatenate([c, c_ctx[None, :], jnp.zeros((bp - b - 1, d), F32)], axis=0)
    mod = _ada(cc, ada_w, ada_b)
    cosp, sinp = _rope_tables(c_len, seq)
    wgu_all = moe_w_gate_up.astype(BF16).reshape((-1,) + moe_w_gate_up.shape[2:])
    wd_all = moe_w_down.astype(BF16).reshape((-1,) + moe_w_down.shape[2:])

    moe = None
    for i in range(depth):
        j = i // N_MIXERS
        mod_c = jnp.broadcast_to(mod[i, b][None, :], (b, ADA_CHUNKS * d))
        modt = jnp.stack([mod_c, mod[i, :b]], axis=1).reshape(2 * b, 1, ADA_CHUNKS * d)
        router = _router_weights(moe_w_group[i], moe_b_group[i], moe_w_expert[i], moe_b_expert[i])
        if i % N_MIXERS == 0:
            xc, ht = _norm_mod(xc, norm_mix_g[i], modt, tl, n_ctx_tiles, True, moe)
            abr, abi, bbr, bbi = _s5_discretize(s5_a_re[j], s5_a_im[j], s5_log_dt[j], s5_b_re[j], s5_b_im[j])
            bm, cm, a = _s5_block_mats(abr, abi, bbr, bbi, s5_c_re[j], s5_c_im[j], gpb)
            yf, yr = _s5_scan(ht.reshape(c_len + seq, b, d), bm, cm, a, c_len, tc)
            flat = lambda y: y.reshape(c_len + seq, b * d)
            xc, h2, route, counts = _glu_tail(xc, flat(yf), flat(yr), ht, s5_d[j], s5_w_glu[j], modt,
                                              norm_ffn_g[i], router, tl, n_ctx_tiles, n_groups, n_exp)
        else:
            xc, h = _norm_mod(xc, norm_mix_g[i], modt, tl, n_ctx_tiles, False, moe)
            weights = _mla_weights(mla_w_in[j], mla_w_uq[j], mla_w_ukv[j], mla_g_q.shape[-1], mla_g_kv.shape[-1])
            q, k, vt = _mla_proj(h, weights, mla_g_q[j], mla_g_kv[j], cosp, sinp, tl)
            o = _attention(q, k, vt, tl, c_len)
            xc, h2, route, counts = _oproj_tail(xc, o, mla_w_o[j], modt, norm_ffn_g[i], router,
                                                tl, n_ctx_tiles, n_groups, n_exp)
        moe = (_moe(h2, route, counts, wgu_all, wd_all, i, tm, n_groups, n_exp), modt)
    return _final_norm(xc, moe[0], moe[1], final_g, tl, n_ctx_tiles, seq)
```

```python
import functools
import math

import jax
import jax.numpy as jnp
import numpy as np
from jax import lax
from jax.experimental import pallas as pl
from jax.experimental.pallas import tpu as pltpu

F32 = jnp.float32
BF16 = jnp.bfloat16
I32 = jnp.int32
U32 = jnp.uint32

EPS = 1e-6
GRID_W = 64
ROPE_THETA = 10000.0
ADA_CHUNKS = 6
N_MIXERS = 2
MLA_HEADS = 8
MLA_NOPE = 128
MLA_ROPE = 64
MLA_V = 128
MOE_TOP_K = 2

LANES = 128
SUBLANES = 8
MXU_DIM = 256
VMEM_LIMIT = 56 * 1024 * 1024
NEG_BIG = -1e30
NT_DIMS = (((1,), (1,)), ((), ()))
HIGH_HALF = np.uint32(0xFFFF0000)
MOE_TILE = 128
MOE_RING = 3
ATTN_KEY_CHUNK = 256


def _cparams(n_axes, vmem=VMEM_LIMIT):
    return pltpu.CompilerParams(dimension_semantics=("arbitrary",) * n_axes, vmem_limit_bytes=vmem)


def _sigmoid(x):
    return 1.0 / (1.0 + jnp.exp(-x))


def _rms(x, g):
    return x * lax.rsqrt(jnp.mean(x * x, axis=-1, keepdims=True) + EPS) * g


def _token_tile(c_len, seq):
    t = 256
    while c_len % t or seq % t:
        t //= 2
    return t


def _pad_lanes(w):
    return jnp.concatenate([w, jnp.zeros(w.shape[:-1] + (LANES - w.shape[-1],), w.dtype)], axis=-1)


def _ada_kernel(cc_ref, w_ref, b_ref, o_ref):
    cc = cc_ref[...]
    s = cc * _sigmoid(cc)
    o_ref[0] = jnp.dot(s, w_ref[0], preferred_element_type=F32, precision=lax.Precision.HIGHEST) + b_ref[0]


def _ada(cc, ada_w, ada_b):
    depth, d, n = ada_w.shape
    bp = cc.shape[0]
    tn = n // 4
    return pl.pallas_call(
        _ada_kernel,
        grid=(depth, n // tn),
        in_specs=[pl.BlockSpec((bp, d), lambda i, j: (0, 0)),
                  pl.BlockSpec((1, d, tn), lambda i, j: (i, 0, j)),
                  pl.BlockSpec((1, 1, tn), lambda i, j: (i, 0, j))],
        out_specs=pl.BlockSpec((1, bp, tn), lambda i, j: (i, 0, j)),
        out_shape=jax.ShapeDtypeStruct((depth, bp, n), F32),
        compiler_params=_cparams(2),
        name="ada_mod",
    )(cc, ada_w, ada_b.reshape(depth, 1, n))


def _mod_spec(chunk, d, n_ctx_tiles, shift=0):
    return pl.BlockSpec((1, 1, d), lambda b, l: (2 * b + jnp.where(l + shift >= n_ctx_tiles, 1, 0), 0, chunk))


def _norm_mod_kernel(x_ref, g_ref, sh_ref, sc_ref, o_ref):
    y = _rms(x_ref[0], g_ref[...])
    o_ref[...] = (y * (1.0 + sc_ref[0]) + sh_ref[0]).astype(o_ref.dtype).reshape(o_ref.shape)


def _token_rows(y_ref, tl):
    return jnp.concatenate([y_ref[pl.ds(k, tl, stride=SUBLANES), :] for k in range(SUBLANES)], axis=-1)


def _res_norm_mod_kernel(x_ref, y_ref, g2_ref, g_ref, sh_ref, sc_ref, xo_ref, o_ref):
    xn = x_ref[0] + g2_ref[0] * _token_rows(y_ref, x_ref.shape[1])
    xo_ref[0] = xn
    o_ref[...] = (_rms(xn, g_ref[...]) * (1.0 + sc_ref[0]) + sh_ref[0]).astype(o_ref.dtype).reshape(o_ref.shape)


def _norm_mod(xc, g, modt, tl, n_ctx_tiles, time_major, moe=None):
    b, lt, d = xc.shape
    nt = lt // tl
    tok = pl.BlockSpec((1, tl, d), lambda bi, l: (bi, l, 0))
    if time_major:
        h_shape = jax.ShapeDtypeStruct((lt, b * d), BF16)
        h_spec = pl.BlockSpec((tl, d), lambda bi, l: (l, bi))
    else:
        h_shape, h_spec = jax.ShapeDtypeStruct((b, lt, d), BF16), tok
    tail_specs = [pl.BlockSpec((1, d), lambda bi, l: (0, 0)),
                  _mod_spec(0, d, n_ctx_tiles), _mod_spec(1, d, n_ctx_tiles)]
    if moe is None:
        h = pl.pallas_call(
            _norm_mod_kernel, grid=(b, nt), in_specs=[tok] + tail_specs, out_specs=h_spec, out_shape=h_shape,
            compiler_params=_cparams(2), name="norm_mod",
        )(xc, g.reshape(1, d), modt, modt)
        return xc, h
    y, prev_modt = moe
    return pl.pallas_call(
        _res_norm_mod_kernel, grid=(b, nt),
        in_specs=[tok, pl.BlockSpec((SUBLANES * tl, LANES), lambda bi, l: (bi * nt + l, 0)), _mod_spec(5, d, n_ctx_tiles)] + tail_specs,
        out_specs=[tok, h_spec],
        out_shape=[jax.ShapeDtypeStruct((b, lt, d), F32), h_shape],
        input_output_aliases={0: 0},
        compiler_params=_cparams(2), name="moe_res_norm_mod",
    )(xc, y, prev_modt, g.reshape(1, d), modt, modt)


def _final_norm_kernel(x_ref, y_ref, g2_ref, g_ref, o_ref):
    o_ref[0] = _rms(x_ref[0] + g2_ref[0] * _token_rows(y_ref, x_ref.shape[1]), g_ref[...])


def _final_norm(xc, y, modt, g, tl, n_ctx_tiles, seq):
    b, lt, d = xc.shape
    nt = lt // tl
    return pl.pallas_call(
        _final_norm_kernel,
        grid=(b, seq // tl),
        in_specs=[pl.BlockSpec((1, tl, d), lambda bi, l: (bi, l + n_ctx_tiles, 0)),
                  pl.BlockSpec((SUBLANES * tl, LANES), lambda bi, l: (bi * nt + l + n_ctx_tiles, 0)),
                  _mod_spec(5, d, n_ctx_tiles, shift=n_ctx_tiles),
                  pl.BlockSpec((1, d), lambda bi, l: (0, 0))],
        out_specs=pl.BlockSpec((1, tl, d), lambda bi, l: (bi, l, 0)),
        out_shape=jax.ShapeDtypeStruct((b, seq, d), F32),
        compiler_params=_cparams(2),
        name="final_norm",
    )(xc, y, modt, g.reshape(1, d))


def _s5_disc_kernel(ar_ref, ai_ref, ldt_ref, br_ref, bi_ref, abr_ref, abi_ref, bbr_ref, bbi_ref):
    ar, ai = ar_ref[...], ai_ref[...]
    dt = jnp.exp(ldt_ref[...])
    ldr, ldi = ar * dt, ai * dt
    mag = jnp.exp(ldr)
    abr, abi = mag * jnp.cos(ldi), mag * jnp.sin(ldi)
    nr, ni = abr - 1.0, abi
    den = ar * ar + ai * ai
    qr = (nr * ar + ni * ai) / den
    qi = (ni * ar - nr * ai) / den
    br, bi = br_ref[...], bi_ref[...]
    abr_ref[...] = abr
    abi_ref[...] = abi
    bbr_ref[...] = qr * br - qi * bi
    bbi_ref[...] = qr * bi + qi * br


def _s5_discretize(a_re, a_im, log_dt, b_re, b_im):
    two, g, p, c = b_re.shape
    rows, cols = two * g, p * c
    wide = lambda a: jnp.broadcast_to(a[..., None], (two, g, p, c)).reshape(rows, cols)
    ldt = jnp.broadcast_to(log_dt[:, :, None, None], (two, g, p, c)).reshape(rows, cols)
    spec = pl.BlockSpec((rows, cols), lambda: (0, 0))
    shape = jax.ShapeDtypeStruct((rows, cols), F32)
    abr, abi, bbr, bbi = pl.pallas_call(
        _s5_disc_kernel,
        in_specs=[spec] * 5,
        out_specs=[spec] * 4,
        out_shape=[shape] * 4,
        name="s5_disc",
    )(wide(a_re), wide(a_im), ldt, b_re.reshape(rows, cols), b_im.reshape(rows, cols))
    pick = lambda a: a.reshape(two, g, p, c)[..., 0]
    return pick(abr), pick(abi), bbr.reshape(two, g, p, c), bbi.reshape(two, g, p, c)


def _s5_block_mats(abr, abi, bbr, bbi, c_re, c_im, gpb):
    two, g, p, c = bbr.shape
    ncb = g // gpb
    eye = jnp.eye(gpb, dtype=F32)

    def in_mat(bb):
        bb = bb.reshape(two, ncb, gpb, p, c)
        return jnp.einsum('dngpc,gh->dngchp', bb, eye).reshape(two, ncb, gpb * c, gpb * p)

    def out_mat(cc):
        cc = cc.reshape(two, ncb, gpb, c, p)
        return jnp.einsum('dngcp,gh->dngphc', cc, eye).reshape(two, ncb, gpb * p, gpb * c)

    bm = jnp.concatenate([in_mat(bbr), in_mat(bbi)], axis=-1).astype(BF16)
    cm = jnp.concatenate([out_mat(c_re), -out_mat(c_im)], axis=-2).astype(BF16)
    a = jnp.stack([abr.reshape(two, ncb, gpb * p), abi.reshape(two, ncb, gpb * p)], axis=2)
    return bm, cm, a


def _s5_kernel(xf_ref, xr_ref, bm_ref, cm_ref, a_ref, yf_ref, yr_ref, buf_ref, bur_ref, st_ref,
               *, tc, nb, cw, ns, rb, lw):
    @pl.when(pl.program_id(1) == 0)
    def _():
        st_ref[...] = jnp.zeros_like(st_ref)

    tpb = rb // nb
    nrb = tc // tpb
    refs = ((xf_ref, yf_ref, buf_ref), (xr_ref, yr_ref, bur_ref))
    order = (tuple(range(nrb)), tuple(range(nrb - 1, -1, -1)))

    def inject(d, r):
        x_ref, _, bu_ref = refs[d]
        x = x_ref[r * tpb:(r + 1) * tpb].reshape(rb, cw)
        bu_ref[r * rb:(r + 1) * rb, :] = jnp.dot(x, bm_ref[d, 0], preferred_element_type=F32)

    def recur(d, r):
        bu_ref = refs[d][2]
        for lc in range(ns // lw):
            lo = lc * lw
            are = jnp.broadcast_to(a_ref[d, 0, 0:1, lo:lo + lw], (nb, lw))
            aim = jnp.broadcast_to(a_ref[d, 0, 1:2, lo:lo + lw], (nb, lw))
            hr, hi = st_ref[d, 0, :, lo:lo + lw], st_ref[d, 1, :, lo:lo + lw]
            for i in range(tpb):
                t = r * tpb + (i if d == 0 else tpb - 1 - i)
                rows = slice(t * nb, (t + 1) * nb)
                hr, hi = (are * hr - aim * hi + bu_ref[rows, lo:lo + lw],
                          are * hi + aim * hr + bu_ref[rows, ns + lo:ns + lo + lw])
                bu_ref[rows, lo:lo + lw] = hr
                bu_ref[rows, ns + lo:ns + lo + lw] = hi
            st_ref[d, 0, :, lo:lo + lw] = hr
            st_ref[d, 1, :, lo:lo + lw] = hi

    def readout(d, r):
        _, y_ref, bu_ref = refs[d]
        h = bu_ref[r * rb:(r + 1) * rb, :].astype(BF16)
        y = jnp.dot(h, cm_ref[d, 0], preferred_element_type=F32)
        y_ref[r * tpb:(r + 1) * tpb] = y.reshape(tpb, nb, cw).astype(y_ref.dtype)

    for d in range(2):
        inject(d, order[d][0])
    for k in range(nrb):
        for d in range(2):
            if k + 1 < nrb:
                inject(d, order[d][k + 1])
        for d in range(2):
            recur(d, order[d][k])
        for d in range(2):
            if k >= 1:
                readout(d, order[d][k - 1])
    for d in range(2):
        readout(d, order[d][nrb - 1])


def _s5_scan(ht3, bm, cm, a, c_len, tc):
    lt, nb, d = ht3.shape
    _, ncb, cw, ns2 = bm.shape
    ns = ns2 // 2
    n_c, n_all = c_len // tc, lt // tc
    rb = min(256, tc * nb)

    def rev(k):
        return jnp.where(k < n_c, n_c - 1 - k, n_all - 1 - (k - n_c))

    x_blk = (tc, nb, cw)
    kern = functools.partial(_s5_kernel, tc=tc, nb=nb, cw=cw, ns=ns, rb=rb, lw=min(256, ns))
    return pl.pallas_call(
        kern,
        grid=(ncb, n_all),
        in_specs=[pl.BlockSpec(x_blk, lambda j, k: (k, 0, j)),
                  pl.BlockSpec(x_blk, lambda j, k: (rev(k), 0, j)),
                  pl.BlockSpec((2, 1, cw, ns2), lambda j, k: (0, j, 0, 0)),
                  pl.BlockSpec((2, 1, ns2, cw), lambda j, k: (0, j, 0, 0)),
                  pl.BlockSpec((2, 1, 2, ns), lambda j, k: (0, j, 0, 0))],
        out_specs=[pl.BlockSpec(x_blk, lambda j, k: (k, 0, j)),
                   pl.BlockSpec(x_blk, lambda j, k: (rev(k), 0, j))],
        out_shape=[jax.ShapeDtypeStruct((lt, nb, d), BF16)] * 2,
        scratch_shapes=[pltpu.VMEM((tc * nb, ns2), F32), pltpu.VMEM((tc * nb, ns2), F32),
                        pltpu.VMEM((2, 2, nb, ns), F32)],
        compiler_params=_cparams(2),
        name="s5_scan",
    )(ht3, ht3, bm, cm, a)


def _route_tail(xn, g_ref, sh_ref, sc_ref, wrh_ref, wrl_ref, br_ref, run_ref, xo_ref, h2_ref, route_ref, cnt_ref,
                *, n_groups, n_exp):
    tl, d = xn.shape
    xo_ref[0] = xn
    h2 = _rms(xn, g_ref[...]) * (1.0 + sc_ref[0]) + sh_ref[0]
    h_hi = h2.astype(BF16)
    h_lo = (h2 - h_hi.astype(F32)).astype(BF16)
    dg = lambda w, x: lax.dot_general(w, x, NT_DIMS, preferred_element_type=F32)
    logits = dg(wrh_ref[...], h_hi) + (dg(wrh_ref[...], h_lo) + dg(wrl_ref[...], h_hi)) + br_ref[...]

    row = lax.broadcasted_iota(I32, (SUBLANES, tl), 0)

    def softmax0(v, mask):
        v = jnp.where(mask, v, NEG_BIG)
        e = jnp.where(mask, jnp.exp(v - jnp.max(v, axis=0, keepdims=True)), 0.0)
        return e / jnp.sum(e, axis=0, keepdims=True)

    def top1(p, mask):
        v = jnp.max(jnp.where(mask, p, -1.0), axis=0, keepdims=True)
        idx = jnp.min(jnp.where(mask & (p == v), row, SUBLANES), axis=0, keepdims=True)
        return v, idx

    is_g = row < n_groups
    p_g, g_idx = top1(softmax0(logits[0:SUBLANES], is_g), is_g)
    le = logits[SUBLANES:2 * SUBLANES]
    for g in range(1, n_groups):
        le = jnp.where(g_idx == g, logits[(g + 1) * SUBLANES:(g + 2) * SUBLANES], le)
    every = row >= 0
    pe = softmax0(le, every)
    p0, i0 = top1(pe, every)
    p1, i1 = top1(pe, row != i0)
    den = p0 + p1
    w0, w1 = p_g * (p0 / den), p_g * (p1 / den)
    first_low = i0 < i1
    ea, eb = jnp.minimum(i0, i1), jnp.maximum(i0, i1)
    wa, wb = jnp.where(first_low, w0, w1), jnp.where(first_low, w1, w0)
    n_pairs = n_exp * (n_exp - 1) // 2
    pair = ea * (n_exp - 1) - jnp.right_shift(ea * (ea - 1), 1) + (eb - ea - 1)
    cls = g_idx * n_pairs + pair

    crow = lax.broadcasted_iota(I32, (LANES, tl), 0)
    hit = crow == cls
    onehot = jnp.where(hit, 1.0, 0.0)
    src = lax.broadcasted_iota(I32, (tl, tl), 0)
    tgt = lax.broadcasted_iota(I32, (tl, tl), 1)
    earlier = jnp.where(src < tgt, 1.0, 0.0).astype(BF16)
    before = jnp.dot(onehot.astype(BF16), earlier, preferred_element_type=F32) + run_ref[...]
    rank = jnp.sum(jnp.where(hit, before, 0.0), axis=0, keepdims=True)
    run_ref[...] = run_ref[...] + jnp.sum(onehot, axis=1, keepdims=True)
    cnt_ref[...] = jnp.broadcast_to(run_ref[...], cnt_ref.shape)

    route = jnp.zeros((SUBLANES, tl), F32)
    for i, v in enumerate((cls.astype(F32), rank, wa, wb)):
        route = jnp.where(row == i, v, route)
    route_ref[0] = route

    diag = src == tgt
    wa_col = jnp.sum(jnp.where(diag, wa, 0.0), axis=1, keepdims=True)
    wb_col = jnp.sum(jnp.where(diag, wb, 0.0), axis=1, keepdims=True)
    lane = lax.broadcasted_iota(I32, (tl, LANES), 1)
    wrow = jnp.where(lane == 0, wa_col, jnp.where(lane == 1, wb_col, 0.0))
    bits = lax.bitcast_convert_type(h_hi.astype(F32), U32)
    half = d // 2
    packed = (bits[:, 0:half] >> 16) | (bits[:, half:d] & HIGH_HALF)
    n_packed = half // LANES
    for c in range(n_packed):
        h2_ref[pl.ds(c, tl, stride=SUBLANES), :] = packed[:, c * LANES:(c + 1) * LANES]
    h2_ref[pl.ds(n_packed, tl, stride=SUBLANES), :] = lax.bitcast_convert_type(wrow, U32)
    for c in range(n_packed + 1, SUBLANES):
        h2_ref[pl.ds(c, tl, stride=SUBLANES), :] = jnp.zeros((tl, LANES), U32)


def _route_rows(n_groups):
    return -(-(SUBLANES * (n_groups + 1)) // 16) * 16


def _tail_specs(b, lt, d, tl, n_ctx_tiles, n_groups):
    nt = lt // tl
    rr = _route_rows(n_groups)
    const = lambda shape: pl.BlockSpec(shape, lambda bi, l: (0,) * len(shape))
    in_specs = [const((1, d)),
                _mod_spec(3, d, n_ctx_tiles),
                _mod_spec(4, d, n_ctx_tiles),
                const((rr, d)), const((rr, d)), const((rr, 1))]
    out_specs = [pl.BlockSpec((1, tl, d), lambda bi, l: (bi, l, 0)),
                 pl.BlockSpec((SUBLANES * tl, LANES), lambda bi, l: (bi * nt + l, 0)),
                 pl.BlockSpec((1, SUBLANES, tl), lambda bi, l: (bi * nt + l, 0, 0)),
                 const((LANES, LANES))]
    out_shape = [jax.ShapeDtypeStruct((b, lt, d), F32),
                 jax.ShapeDtypeStruct((SUBLANES * b * lt, LANES), U32),
                 jax.ShapeDtypeStruct((b * nt, SUBLANES, tl), F32),
                 jax.ShapeDtypeStruct((LANES, LANES), F32)]
    return in_specs, out_specs, out_shape


def _init_run(run_ref):
    @pl.when((pl.program_id(0) == 0) & (pl.program_id(1) == 0))
    def _():
        run_ref[...] = jnp.zeros_like(run_ref)


def _router_weights(w_group, b_group, w_expert, b_expert):
    d, n_groups = w_group.shape
    n_exp = w_expert.shape[-1]
    rr = _route_rows(n_groups)
    w = jnp.zeros((rr, d), F32).at[0:n_groups].set(w_group.T)
    w = w.at[SUBLANES:SUBLANES * (n_groups + 1)].set(jnp.transpose(w_expert, (0, 2, 1)).reshape(n_groups * n_exp, d))
    bias = jnp.zeros((rr, 1), F32).at[0:n_groups, 0].set(b_group)
    bias = bias.at[SUBLANES:SUBLANES * (n_groups + 1), 0].set(b_expert.reshape(-1))
    w_hi = w.astype(BF16)
    return w_hi, (w - w_hi.astype(F32)).astype(BF16), bias


def _glu_kernel(x_ref, yf_ref, yr_ref, h_ref, dd_ref, w_ref, g1_ref,
                g_ref, sh_ref, sc_ref, wrh_ref, wrl_ref, br_ref,
                xo_ref, h2_ref, route_ref, cnt_ref, run_ref, *, n_groups, n_exp):
    _init_run(run_ref)
    d = x_ref.shape[-1]
    u = yf_ref[...].astype(F32) + yr_ref[...].astype(F32) + dd_ref[...] * h_ref[...].astype(F32)
    z = 0.5 * u * (1.0 + jnp.tanh(math.sqrt(2.0 / math.pi) * (u + 0.044715 * (u * u * u))))
    o = jnp.dot(z.astype(BF16), w_ref[...], preferred_element_type=F32)
    mix = o[:, :d] * _sigmoid(o[:, d:])
    xn = x_ref[0] + g1_ref[0] * mix
    _route_tail(xn, g_ref, sh_ref, sc_ref, wrh_ref, wrl_ref, br_ref, run_ref, xo_ref, h2_ref, route_ref, cnt_ref,
                n_groups=n_groups, n_exp=n_exp)


def _glu_tail(xc, yft, yrt, ht, s5_d, w_glu, modt, g_ffn, router, tl, n_ctx_tiles, n_groups, n_exp):
    b, lt, d = xc.shape
    tm_spec = pl.BlockSpec((tl, d), lambda bi, l: (l, bi))
    t_in, t_out, t_shape = _tail_specs(b, lt, d, tl, n_ctx_tiles, n_groups)
    return pl.pallas_call(
        functools.partial(_glu_kernel, n_groups=n_groups, n_exp=n_exp),
        grid=(b, lt // tl),
        in_specs=[pl.BlockSpec((1, tl, d), lambda bi, l: (bi, l, 0)), tm_spec, tm_spec, tm_spec,
                  pl.BlockSpec((1, d), lambda bi, l: (0, 0)),
                  pl.BlockSpec((d, 2 * d), lambda bi, l: (0, 0)),
                  _mod_spec(2, d, n_ctx_tiles)] + t_in,
        out_specs=t_out,
        out_shape=t_shape,
        scratch_shapes=[pltpu.VMEM((LANES, 1), F32)],
        input_output_aliases={0: 0},
        compiler_params=_cparams(2),
        name="s5_glu_router",
    )(xc, yft, yrt, ht, s5_d.reshape(1, d), w_glu.astype(BF16), modt,
      g_ffn.reshape(1, d), modt, modt, *router)


def _mla_proj_kernel(h_ref, w1_ref, gq_ref, gkv_ref, w2_ref, w2r_ref, wk_ref, wvt_ref, cos_ref, sin_ref,
                     q_ref, k_ref, vt_ref, *, q_rank, kv_rank, n_heads, scale):
    p = jnp.dot(h_ref[0], w1_ref[...], preferred_element_type=F32)
    o_kr = q_rank + kv_rank
    cqn = _rms(p[:, :q_rank], gq_ref[...]).astype(BF16)
    ckvn = _rms(p[:, q_rank:o_kr], gkv_ref[...]).astype(BF16)
    cosp, sinp = cos_ref[...], sin_ref[...]
    kr = (p[:, o_kr:o_kr + LANES] * cosp + p[:, o_kr + LANES:o_kr + 2 * LANES] * sinp).astype(BF16)
    qm = jnp.dot(cqn, w2_ref[...], preferred_element_type=F32)
    qrot = jnp.dot(cqn, w2r_ref[...], preferred_element_type=F32)
    kn = jnp.dot(ckvn, wk_ref[...], preferred_element_type=F32)
    vt = lax.dot_general(wvt_ref[...], ckvn, NT_DIMS, preferred_element_type=F32)
    hn = n_heads * MLA_NOPE
    for h in range(n_heads):
        s = slice(h * LANES, (h + 1) * LANES)
        sr = slice(hn + h * LANES, hn + (h + 1) * LANES)
        q_ref[0, h, :, 0:LANES] = (qm[:, s] * scale).astype(BF16)
        q_ref[0, h, :, LANES:2 * LANES] = ((qm[:, sr] * cosp + qrot[:, s] * sinp) * scale).astype(BF16)
        k_ref[0, h, :, 0:LANES] = kn[:, s].astype(BF16)
        k_ref[0, h, :, LANES:2 * LANES] = kr
        vt_ref[0, h] = vt[h * MLA_V:(h + 1) * MLA_V].astype(BF16)


def _rot_cols(w):
    a1, a2, b1, b2 = jnp.split(w, 4, axis=-1)
    return jnp.concatenate([-a2, a1, -b2, b1], axis=-1)


def _mla_weights(w_in, w_uq, w_ukv, q_rank, kv_rank):
    h = MLA_HEADS
    o_kr = q_rank + kv_rank
    w_kr = w_in[:, o_kr:]
    w1 = jnp.concatenate([w_in[:, :o_kr], _pad_lanes(w_kr), _pad_lanes(_rot_cols(w_kr))], axis=-1)
    uq = w_uq.reshape(q_rank, h, MLA_NOPE + MLA_ROPE)
    uq_n, uq_r = uq[..., :MLA_NOPE], uq[..., MLA_NOPE:]
    w2 = jnp.concatenate([uq_n.reshape(q_rank, h * MLA_NOPE), _pad_lanes(uq_r).reshape(q_rank, h * LANES)], axis=-1)
    w2r = _pad_lanes(_rot_cols(uq_r)).reshape(q_rank, h * LANES)
    ukv = w_ukv.reshape(kv_rank, h, MLA_NOPE + MLA_V)
    wk = ukv[..., :MLA_NOPE].reshape(kv_rank, h * MLA_NOPE)
    wvt = ukv[..., MLA_NOPE:].reshape(kv_rank, h * MLA_V).T
    return [w.astype(BF16) for w in (w1, w2, w2r, wk, wvt)]


def _rope_tables(c_len, seq):
    half = MLA_ROPE // 2
    inv_freq = 1.0 / (ROPE_THETA ** (jnp.arange(0, half, 2, dtype=F32) / half))
    pos = jnp.arange(seq, dtype=I32)
    ang_r = (pos // GRID_W).astype(F32)[:, None] * inv_freq
    ang_c = (pos % GRID_W).astype(F32)[:, None] * inv_freq
    ang = jnp.concatenate([ang_r, ang_r, ang_c, ang_c], axis=-1)
    ang = jnp.concatenate([jnp.zeros((c_len, MLA_ROPE), F32), ang], axis=0)
    return _pad_lanes(jnp.cos(ang)), _pad_lanes(jnp.sin(ang))


def _mla_proj(h, weights, g_q, g_kv, cosp, sinp, tl):
    b, lt, d = h.shape
    w1, w2, w2r, wk, wvt = weights
    q_rank, kv_rank = g_q.shape[0], g_kv.shape[0]
    nh = MLA_HEADS
    full = lambda a: pl.BlockSpec(a.shape, lambda bi, l: (0,) * a.ndim)
    scale = math.log2(math.e) / math.sqrt(MLA_NOPE + MLA_ROPE)
    return pl.pallas_call(
        functools.partial(_mla_proj_kernel, q_rank=q_rank, kv_rank=kv_rank, n_heads=nh, scale=scale),
        grid=(b, lt // tl),
        in_specs=[pl.BlockSpec((1, tl, d), lambda bi, l: (bi, l, 0)), full(w1),
                  pl.BlockSpec((1, q_rank), lambda bi, l: (0, 0)),
                  pl.BlockSpec((1, kv_rank), lambda bi, l: (0, 0)),
                  full(w2), full(w2r), full(wk), full(wvt),
                  pl.BlockSpec((tl, LANES), lambda bi, l: (l, 0)),
                  pl.BlockSpec((tl, LANES), lambda bi, l: (l, 0))],
        out_specs=[pl.BlockSpec((1, nh, tl, 2 * LANES), lambda bi, l: (bi, 0, l, 0)),
                   pl.BlockSpec((1, nh, tl, 2 * LANES), lambda bi, l: (bi, 0, l, 0)),
                   pl.BlockSpec((1, nh, MLA_V, tl), lambda bi, l: (bi, 0, 0, l))],
        out_shape=[jax.ShapeDtypeStruct((b, nh, lt, 2 * LANES), BF16),
                   jax.ShapeDtypeStruct((b, nh, lt, 2 * LANES), BF16),
                   jax.ShapeDtypeStruct((b, nh, MLA_V, lt), BF16)],
        compiler_params=_cparams(2),
        name="mla_proj",
    )(h, w1, g_q.reshape(1, q_rank), g_kv.reshape(1, kv_rank), w2, w2r, wk, wvt, cosp, sinp)


def _attn_kernel(q_ref, k_ref, vt_ref, o_ref, s0_ref, s1_ref, m0_ref, m1_ref, *, n_ctx_tiles, c_len, n_heads, ck):
    def run(lk):
        def scores(h, s_ref, m_ref):
            q = q_ref[0, h]
            m = None
            for c0 in range(0, lk, ck):
                s = lax.dot_general(k_ref[0, h, c0:c0 + ck, :], q, NT_DIMS, preferred_element_type=F32)
                s_ref[c0:c0 + ck, :] = s
                cm = jnp.max(s, axis=0, keepdims=True)
                m = cm if m is None else jnp.maximum(m, cm)
            m_ref[...] = m

        def apply_v(h, s_ref, m_ref):
            p = jnp.exp2(s_ref[0:lk, :] - m_ref[...])
            den = jnp.sum(p, axis=0, keepdims=True)
            acc = jnp.dot(vt_ref[0, h, :, 0:lk], p.astype(BF16), preferred_element_type=F32)
            o_ref[0, h] = (acc / den).T.astype(o_ref.dtype)

        def head_pair(i, carry):
            h0 = 2 * i
            scores(h0 + 1, s1_ref, m1_ref)
            apply_v(h0, s0_ref, m0_ref)
            scores(jnp.minimum(h0 + 2, n_heads - 2), s0_ref, m0_ref)
            apply_v(h0 + 1, s1_ref, m1_ref)
            return carry

        scores(0, s0_ref, m0_ref)
        lax.fori_loop(0, n_heads // 2, head_pair, 0)

    qi = pl.program_id(1)

    @pl.when(qi < n_ctx_tiles)
    def _():
        run(c_len)

    @pl.when(qi >= n_ctx_tiles)
    def _():
        run(k_ref.shape[2])


def _attention(q, k, vt, tl, c_len):
    b, nh, lt, dk = q.shape
    ck = min(ATTN_KEY_CHUNK, c_len)
    return pl.pallas_call(
        functools.partial(_attn_kernel, n_ctx_tiles=c_len // tl, c_len=c_len, n_heads=nh, ck=ck),
        grid=(b, lt // tl),
        in_specs=[pl.BlockSpec((1, nh, tl, dk), lambda bi, l: (bi, 0, l, 0)),
                  pl.BlockSpec((1, nh, lt, dk), lambda bi, l: (bi, 0, 0, 0)),
                  pl.BlockSpec((1, nh, MLA_V, lt), lambda bi, l: (bi, 0, 0, 0))],
        out_specs=pl.BlockSpec((1, nh, tl, MLA_V), lambda bi, l: (bi, 0, l, 0)),
        out_shape=jax.ShapeDtypeStruct((b, nh, lt, MLA_V), BF16),
        scratch_shapes=[pltpu.VMEM((lt, tl), F32), pltpu.VMEM((lt, tl), F32),
                        pltpu.VMEM((1, tl), F32), pltpu.VMEM((1, tl), F32)],
        compiler_params=_cparams(2),
        name="mla_attn",
    )(q, k, vt)


def _oproj_kernel(x_ref, o_ref, w_ref, g1_ref, g_ref, sh_ref, sc_ref, wrh_ref, wrl_ref, br_ref,
                  xo_ref, h2_ref, route_ref, cnt_ref, run_ref, *, n_heads, n_groups, n_exp):
    _init_run(run_ref)
    o = jnp.concatenate([o_ref[0, h] for h in range(n_heads)], axis=-1)
    mix = jnp.dot(o, w_ref[...], preferred_element_type=F32)
    xn = x_ref[0] + g1_ref[0] * mix
    _route_tail(xn, g_ref, sh_ref, sc_ref, wrh_ref, wrl_ref, br_ref, run_ref, xo_ref, h2_ref, route_ref, cnt_ref,
                n_groups=n_groups, n_exp=n_exp)


def _oproj_tail(xc, o, w_o, modt, g_ffn, router, tl, n_ctx_tiles, n_groups, n_exp):
    b, lt, d = xc.shape
    nh = o.shape[1]
    t_in, t_out, t_shape = _tail_specs(b, lt, d, tl, n_ctx_tiles, n_groups)
    return pl.pallas_call(
        functools.partial(_oproj_kernel, n_heads=nh, n_groups=n_groups, n_exp=n_exp),
        grid=(b, lt // tl),
        in_specs=[pl.BlockSpec((1, tl, d), lambda bi, l: (bi, l, 0)),
                  pl.BlockSpec((1, nh, tl, MLA_V), lambda bi, l: (bi, 0, l, 0)),
                  pl.BlockSpec(w_o.shape, lambda bi, l: (0, 0)),
                  _mod_spec(2, d, n_ctx_tiles)] + t_in,
        out_specs=t_out,
        out_shape=t_shape,
        scratch_shapes=[pltpu.VMEM((LANES, 1), F32)],
        input_output_aliases={0: 0},
        compiler_params=_cparams(2),
        name="mla_out_router",
    )(xc, o, w_o.astype(BF16), modt, g_ffn.reshape(1, d), modt, modt, *router)


def _for_rows(n, tm, fn):
    @pl.when(n == tm)
    def _():
        for r in range(tm):
            fn(r)

    k = tm // 2
    while k >= 1:
        @pl.when((n < tm) & ((n & k) != 0))
        def _(k=k):
            base = n & ~(2 * k - 1)
            for r in range(k):
                fn(base + r)
        k //= 2


def _expert_kernel(ea_ref, eb_ref, nu_ref, nv_ref, src0_ref, src1_ref, src_ahead_ref, dst_ref, h_ref,
                   wgu_a_ref, wd_a_ref, wgu_b_ref, wd_b_ref, y_ref, xbuf, ybuf, gsem, ssem, *, tm, d):
    del ea_ref, eb_ref
    j = pl.program_id(0)
    n_used = nu_ref[0]
    valid = lambda tile: nv_ref[tile + MOE_RING]
    slot = j % MOE_RING
    ahead = (j + 2) % MOE_RING
    mid = (j + 1) % MOE_RING
    n_packed = d // 2 // LANES
    hid = wd_a_ref.shape[1]

    def tile_rows(idx):
        return pl.multiple_of(idx * SUBLANES, SUBLANES)

    def gather_row(idx, s, r):
        return pltpu.make_async_copy(h_ref.at[pl.ds(tile_rows(idx), n_packed + 1)],
                                     xbuf.at[s, pl.ds(tile_rows(r), n_packed + 1)], gsem.at[s])

    def scatter_row(idx, s, r):
        return pltpu.make_async_copy(ybuf.at[s, pl.ds(tile_rows(r), SUBLANES)],
                                     y_ref.at[pl.ds(tile_rows(idx), SUBLANES)], ssem.at[s])

    def start_gather(idx_ref, s, n):
        _for_rows(n, tm, lambda r: gather_row(idx_ref[0, 0, r], s, r).start(priority=0))

    def start_scatter(idx_ref, s, n):
        _for_rows(n, tm, lambda r: scatter_row(idx_ref[0, 0, r], s, r).start(priority=1))

    def wait_gather(s, n):
        _for_rows(n, tm, lambda r: gather_row(0, s, 0).wait())

    def wait_scatter(s, n):
        _for_rows(n, tm, lambda r: scatter_row(0, s, 0).wait())

    @pl.when(j == 0)
    def _():
        xbuf[...] = jnp.zeros_like(xbuf)
        start_gather(src0_ref, 0, valid(0))
        start_gather(src1_ref, 1, valid(1))

    @pl.when(j <= n_used)
    def _():
        wait_gather(slot, valid(j))
        wait_scatter(slot, valid(j - 3))
        token_row = lambda buf, k: buf[slot, pl.ds(k, tm, stride=SUBLANES), :]
        words = [token_row(xbuf, k) for k in range(n_packed)]
        unpack = lambda w: lax.bitcast_convert_type(w, F32)
        x = jnp.concatenate([unpack(w << 16) for w in words] + [unpack(w & HIGH_HALF) for w in words],
                            axis=-1).astype(BF16)
        wts = unpack(token_row(xbuf, n_packed))
        start_scatter(dst_ref, ahead, valid(j - 1))
        start_gather(src_ahead_ref, ahead, valid(j + 2))

        def ffn(wgu_ref, wd_ref):
            gu = jnp.dot(x, wgu_ref[0], preferred_element_type=F32)
            gate, up = gu[:, :hid], gu[:, hid:]
            act = (gate * _sigmoid(gate) * up).astype(BF16)
            return jnp.dot(act, wd_ref[0], preferred_element_type=F32)

        y = wts[:, 0:1] * ffn(wgu_a_ref, wd_a_ref) + wts[:, 1:2] * ffn(wgu_b_ref, wd_b_ref)
        for k in range(d // LANES):
            ybuf[slot, pl.ds(k, tm, stride=SUBLANES), :] = y[:, k * LANES:(k + 1) * LANES]

        @pl.when(j == n_used)
        def _():
            wait_scatter(mid, valid(j - 2))
            wait_scatter(ahead, valid(j - 1))
            wait_gather(mid, valid(j + 1))
            wait_gather(ahead, valid(j + 2))


def _pair_tables(n_groups, n_exp):
    a, b = np.triu_indices(n_exp, k=1)
    base = (np.arange(n_groups) * n_exp)[:, None]
    return (base + a[None, :]).reshape(-1).astype(np.int32), (base + b[None, :]).reshape(-1).astype(np.int32)


def _moe(h2, route, counts, wgu, wd, layer, tm, n_groups, n_exp):
    t = h2.shape[0] // SUBLANES
    d = wgu.shape[1]
    cls_a, cls_b = _pair_tables(n_groups, n_exp)
    n_cls = cls_a.shape[0]
    n_tiles = t // tm + n_cls
    cls = route[:, 0, :].reshape(t).astype(I32)
    rank = route[:, 1, :].reshape(t).astype(I32)
    cnt = counts[:n_cls, 0].astype(I32)
    padded = ((cnt + tm - 1) // tm) * tm
    seg_end = jnp.cumsum(padded)
    seg_start = seg_end - padded
    is_cls = cls[:, None] == jnp.arange(n_cls, dtype=I32)[None, :]
    pos = jnp.sum(jnp.where(is_cls, seg_start[None, :], 0), axis=1) + rank
    tok = jnp.zeros((n_tiles * tm,), I32).at[pos].set(jnp.arange(t, dtype=I32), unique_indices=True)
    n_used = (seg_end[-1] // tm).astype(I32)
    tile = jnp.arange(n_tiles + 1, dtype=I32)
    tile_cls = jnp.sum((seg_end[None, :] <= (tile * tm)[:, None]).astype(I32), axis=1)
    tile_cls = jnp.minimum(jnp.where(tile < n_used, tile_cls, tile_cls[n_used - 1]), n_cls - 1)
    first = layer * n_groups * n_exp
    ea, eb = first + jnp.asarray(cls_a)[tile_cls], first + jnp.asarray(cls_b)[tile_cls]
    n_valid = jnp.clip((seg_start + cnt)[tile_cls] - tile * tm, 0, tm)
    n_valid = jnp.where(tile < n_used, n_valid, 0)
    ring_pad = jnp.zeros((MOE_RING,), I32)
    n_valid = jnp.concatenate([ring_pad, n_valid, ring_pad])
    src = tok.reshape(n_tiles, 1, tm)
    dst_prev = jnp.concatenate([jnp.zeros((tm,), I32), tok]).reshape(n_tiles + 1, 1, tm)

    two_h = wgu.shape[-1]
    idx_blk = (1, 1, tm)
    smem = lambda f: pl.BlockSpec(idx_blk, f, memory_space=pltpu.SMEM)
    wspec = lambda shape, which: pl.BlockSpec(shape, lambda j, a, bb, nu, nv: ((a, bb)[which][j], 0, 0))
    return pl.pallas_call(
        functools.partial(_expert_kernel, tm=tm, d=d),
        grid_spec=pltpu.PrefetchScalarGridSpec(
            num_scalar_prefetch=4,
            grid=(n_tiles + 1,),
            in_specs=[smem(lambda j, a, bb, nu, nv: (0, 0, 0)),
                      smem(lambda j, a, bb, nu, nv: (min(1, n_tiles - 1), 0, 0)),
                      smem(lambda j, a, bb, nu, nv: (jnp.minimum(j + 2, n_tiles - 1), 0, 0)),
                      smem(lambda j, a, bb, nu, nv: (j, 0, 0)),
                      pl.BlockSpec(memory_space=pl.ANY),
                      wspec((1, d, two_h), 0), wspec((1, two_h // 2, d), 0),
                      wspec((1, d, two_h), 1), wspec((1, two_h // 2, d), 1)],
            out_specs=pl.BlockSpec(memory_space=pl.ANY),
            scratch_shapes=[pltpu.VMEM((MOE_RING, SUBLANES * tm, LANES), U32),
                            pltpu.VMEM((MOE_RING, SUBLANES * tm, LANES), F32),
                            pltpu.SemaphoreType.DMA((MOE_RING,)), pltpu.SemaphoreType.DMA((MOE_RING,))]),
        out_shape=jax.ShapeDtypeStruct((SUBLANES * t, LANES), F32),
        compiler_params=_cparams(1),
        name="moe_experts",
    )(ea, eb, n_used.reshape(1), n_valid, src, src, src, dst_prev, h2, wgu, wd, wgu, wd)


def kernel(x, c, ctx, c_ctx, ada_w, ada_b, norm_mix_g, norm_ffn_g, s5_a_re, s5_a_im, s5_log_dt, s5_b_re, s5_b_im, s5_c_re, s5_c_im, s5_d, s5_w_glu, mla_w_in, mla_g_q, mla_g_kv, mla_w_uq, mla_w_ukv, mla_w_o, moe_w_group, moe_b_group, moe_w_expert, moe_b_expert, moe_w_gate_up, moe_w_down, final_g):
    b, seq, d = x.shape
    c_len = ctx.shape[1]
    depth = ada_w.shape[0]
    n_groups, n_exp = moe_w_group.shape[-1], moe_w_expert.shape[-1]
    assert n_exp == SUBLANES and n_groups < SUBLANES and n_groups * n_exp * (n_exp - 1) // 2 <= LANES
    assert seq % GRID_W == 0 and d == SUBLANES * LANES
    tl = _token_tile(c_len, seq)
    n_ctx_tiles = c_len // tl
    tm = min(MOE_TILE, tl)
    tc = min(64, tl)
    gpb = MXU_DIM // s5_b_re.shape[-1]

    xc = jnp.concatenate([ctx, x], axis=1)
    bp = -(-(b + 1) // SUBLANES) * SUBLANES
    cc = jnp.concatenate([c, c_ctx[None, :], jnp.zeros((bp - b - 1, d), F32)], axis=0)
    mod = _ada(cc, ada_w, ada_b)
    cosp, sinp = _rope_tables(c_len, seq)
    wgu_all = moe_w_gate_up.astype(BF16).reshape((-1,) + moe_w_gate_up.shape[2:])
    wd_all = moe_w_down.astype(BF16).reshape((-1,) + moe_w_down.shape[2:])

    moe = None
    for i in range(depth):
        j = i // N_MIXERS
        mod_c = jnp.broadcast_to(mod[i, b][None, :], (b, ADA_CHUNKS * d))
        modt = jnp.stack([mod_c, mod[i, :b]], axis=1).reshape(2 * b, 1, ADA_CHUNKS * d)
        router = _router_weights(moe_w_group[i], moe_b_group[i], moe_w_expert[i], moe_b_expert[i])
        if i % N_MIXERS == 0:
            xc, ht = _norm_mod(xc, norm_mix_g[i], modt, tl, n_ctx_tiles, True, moe)
            abr, abi, bbr, bbi = _s5_discretize(s5_a_re[j], s5_a_im[j], s5_log_dt[j], s5_b_re[j], s5_b_im[j])
            bm, cm, a = _s5_block_mats(abr, abi, bbr, bbi, s5_c_re[j], s5_c_im[j], gpb)
            yf, yr = _s5_scan(ht.reshape(c_len + seq, b, d), bm, cm, a, c_len, tc)
            flat = lambda y: y.reshape(c_len + seq, b * d)
            xc, h2, route, counts = _glu_tail(xc, flat(yf), flat(yr), ht, s5_d[j], s5_w_glu[j], modt,
                                              norm_ffn_g[i], router, tl, n_ctx_tiles, n_groups, n_exp)
        else:
            xc, h = _norm_mod(xc, norm_mix_g[i], modt, tl, n_ctx_tiles, False, moe)
            weights = _mla_weights(mla_w_in[j], mla_w_uq[j], mla_w_ukv[j], mla_g_q.shape[-1], mla_g_kv.shape[-1])
            q, k, vt = _mla_proj(h, weights, mla_g_q[j], mla_g_kv[j], cosp, sinp, tl)
            o = _attention(q, k, vt, tl, c_len)
            xc, h2, route, counts = _oproj_tail(xc, o, mla_w_o[j], modt, norm_ffn_g[i], router,
                                                tl, n_ctx_tiles, n_groups, n_exp)
        moe = (_moe(h2, route, counts, wgu_all, wd_all, i, tm, n_groups, n_exp), modt)
    return _final_norm(xc, moe[0], moe[1], final_g, tl, n_ctx_tiles, seq)
```
